```python
import math
import jax, jax.numpy as jnp
from jax import lax
import numpy as np

D_MODEL = 1024
BATCH = 16
SEQ = 2048
DEPTH = 2
DEC_BATCH = 128
DEC_SEQ = 4
PAST_LEN = 16384
PAGE_SIZE = 128

N_EVEN = (DEPTH + 1) // 2
N_ODD = DEPTH // 2

SSD_HEAD_DIM = 64
SSD_WIDTH = D_MODEL
SSD_HEADS = SSD_WIDTH // SSD_HEAD_DIM
SSD_GROUPS = 2
SSD_STATE = 128
SSD_CONV = 4
SSD_CHUNK = 128
CONV_DIM = SSD_WIDTH + 2 * SSD_GROUPS * SSD_STATE

ATT_HEAD_DIM = 64
ATT_HEADS = D_MODEL // ATT_HEAD_DIM
ATT_KV_HEADS = 4
ATT_REP = ATT_HEADS // ATT_KV_HEADS
WINDOW = 128
Q_DIM = ATT_HEADS * ATT_HEAD_DIM
KV_DIM = ATT_KV_HEADS * ATT_HEAD_DIM

IN_DIM = SSD_WIDTH + CONV_DIM + SSD_HEADS + Q_DIM + 2 * KV_DIM
MIX_DIM = SSD_WIDTH + Q_DIM

RWKV_HEAD = 64
RWKV_HEADS = D_MODEL // RWKV_HEAD
DECAY_LORA = 64
ICLR_LORA = 64
GATE_LORA = 128
RWKV_GN_EPS = 64e-5

D_FF = 2816
N_EXPERTS = 8
TOP_K = 2
EXPERT_FF = 2816

RMS_EPS = 1e-6
NEG_INF = -1e30

kernel_name = 'hybrid_ssd_swa_rwkv7_moe_step'


def _rms(x, w):
    xf = x.astype(jnp.float32)
    y = xf * lax.rsqrt(jnp.mean(xf * xf, axis=-1, keepdims=True) + RMS_EPS)
    return (y * w.astype(jnp.float32)).astype(x.dtype)


def _swiglu(h, wg, wu, wd):
    return (jax.nn.silu(h @ wg) * (h @ wu)) @ wd


def _causal_conv(xbc, buf, w, b):
    L = xbc.shape[1]
    full = jnp.concatenate([buf.astype(xbc.dtype), xbc], axis=1)
    out = b + sum(full[:, j:j + L] * w[j] for j in range(SSD_CONV))
    return jax.nn.silu(out), full[:, L:]


def _ssd_scan(x, dt, A, Bm, Cm, h0):
    b, L = x.shape[:2]
    q = min(SSD_CHUNK, L)
    pad = (-L) % q
    if pad:
        padfn = lambda t: jnp.pad(t, [(0, 0), (0, pad)] + [(0, 0)] * (t.ndim - 2))
        x, dt, Bm, Cm = padfn(x), padfn(dt), padfn(Bm), padfn(Cm)
    c = (L + pad) // q
    G, R = SSD_GROUPS, SSD_HEADS // SSD_GROUPS
    xs = x.reshape(b, c, q, G, R, SSD_HEAD_DIM)
    dts = dt.reshape(b, c, q, G, R)
    Bs = Bm.reshape(b, c, q, G, SSD_STATE)
    Cs = Cm.reshape(b, c, q, G, SSD_STATE)
    a_cs = jnp.moveaxis(jnp.cumsum(dts * A.reshape(G, R), axis=2), 2, -1)
    causal = jnp.tril(jnp.ones((q, q), bool))
    decay_in = jnp.exp(jnp.where(causal, a_cs[..., :, None] - a_cs[..., None, :], -jnp.inf))
    xdt = xs * dts[..., None]
    cb = jnp.einsum('bcqgn,bckgn->bcgqk', Cs, Bs)
    y_diag = jnp.einsum('bcgqk,bcgrqk,bckgrp->bcqgrp', cb, decay_in, xdt)
    decay_to_end = jnp.exp(a_cs[..., -1:] - a_cs)
    chunk_states = jnp.einsum('bckgn,bcgrk,bckgrp->bcgrpn', Bs, decay_to_end, xdt)
    chunk_decay = jnp.exp(a_cs[..., -1])

    def step(h, inp):
        dec, st = inp
        return h * dec[..., None, None] + st, h

    h_final, h_prev = lax.scan(step, h0.reshape(b, G, R, SSD_HEAD_DIM, SSD_STATE),
                               (jnp.moveaxis(chunk_decay, 1, 0), jnp.moveaxis(chunk_states, 1, 0)))
    h_prev = jnp.moveaxis(h_prev, 0, 1)
    y_off = jnp.einsum('bcqgn,bcgrpn,bcgrq->bcqgrp', Cs, h_prev, jnp.exp(a_cs))
    y = (y_diag + y_off).reshape(b, c * q, SSD_HEADS, SSD_HEAD_DIM)[:, :L]
    return y, h_final.reshape(b, SSD_HEADS, SSD_HEAD_DIM, SSD_STATE)


def _sink_attend(q, k, v, mask, sinks):
    f32 = jnp.float32
    s = jnp.einsum('bnqhrd,bnkhd->bnhrqk', q.astype(f32), k.astype(f32)) * (ATT_HEAD_DIM ** -0.5)
    s = jnp.where(mask[None, :, None, None], s, NEG_INF)
    sink = sinks.astype(f32).reshape(1, 1, ATT_KV_HEADS, ATT_REP, 1, 1)
    m = jnp.maximum(jnp.max(s, axis=-1, keepdims=True), sink)
    p = jnp.exp(s - m)
    p = p / (jnp.sum(p, axis=-1, keepdims=True) + jnp.exp(sink - m))
    o = jnp.einsum('bnhrqk,bnkhd->bnqhrd', p, v.astype(f32))
    b, n, Q = o.shape[:3]
    return o.reshape(b, n * Q, Q_DIM)


def _swa_prompt(q, k, v, sinks):
    b, L = q.shape[:2]
    nb = L // WINDOW
    qb = q.reshape(b, nb, WINDOW, ATT_KV_HEADS, ATT_REP, ATT_HEAD_DIM)

    def band(t):
        tb = t.reshape(b, nb, WINDOW, ATT_KV_HEADS, ATT_HEAD_DIM)
        prev = jnp.concatenate([jnp.zeros_like(tb[:, :1]), tb[:, :-1]], axis=1)
        return jnp.concatenate([prev, tb], axis=2)

    qi = jnp.arange(WINDOW)[:, None]
    kj = jnp.arange(2 * WINDOW)[None, :]
    rel = qi + WINDOW - kj
    blk = jnp.arange(nb)[:, None, None]
    mask = ((rel >= 0) & (rel <= WINDOW))[None] & ((blk > 0) | (kj >= WINDOW)[None])
    return _sink_attend(qb, band(k), band(v), mask, sinks)


def _swa_cached(q, k, v, kbuf, vbuf, sinks):
    b, L = q.shape[:2]
    wc = kbuf.shape[1]
    kc = jnp.concatenate([kbuf.astype(k.dtype), k], axis=1)
    vc = jnp.concatenate([vbuf.astype(v.dtype), v], axis=1)
    qi = jnp.arange(L)[:, None]
    kj = jnp.arange(wc + L)[None, :]
    rel = qi + wc - kj
    mask = ((rel >= 0) & (rel <= WINDOW))[None]
    o = _sink_attend(q.reshape(b, 1, L, ATT_KV_HEADS, ATT_REP, ATT_HEAD_DIM), kc[:, None], vc[:, None], mask, sinks)
    return o, kc[:, -wc:], vc[:, -wc:]


def _ssd_swa_mixer(h, ssm0, conv0, kbuf, vbuf, w_in, conv_w, conv_b, dt_bias, a_log, d_skip,
                   ssd_norm, q_norm, k_norm, sinks, w_out):
    f32 = jnp.float32
    b, L, _ = h.shape
    proj = h @ w_in
    cuts = np.cumsum([SSD_WIDTH, CONV_DIM, SSD_HEADS, Q_DIM, KV_DIM]).tolist()
    z, xbc, dt_raw, q, k, v = jnp.split(proj, cuts, axis=-1)
    xbc, conv1 = _causal_conv(xbc, conv0, conv_w, conv_b)
    xs, Bm, Cm = jnp.split(xbc, [SSD_WIDTH, SSD_WIDTH + SSD_GROUPS * SSD_STATE], axis=-1)
    dt = jax.nn.softplus(dt_raw.astype(f32) + dt_bias.astype(f32))
    A = -jnp.exp(a_log.astype(f32))
    xh = xs.reshape(b, L, SSD_HEADS, SSD_HEAD_DIM).astype(f32)
    y, ssm1 = _ssd_scan(xh, dt, A,
                        Bm.reshape(b, L, SSD_GROUPS, SSD_STATE).astype(f32),
                        Cm.reshape(b, L, SSD_GROUPS, SSD_STATE).astype(f32),
                        ssm0.astype(f32))
    y = (y + xh * d_skip.astype(f32)[:, None]).reshape(b, L, SSD_WIDTH) * jax.nn.silu(z.astype(f32))
    yg = y.reshape(b, L, SSD_GROUPS, SSD_WIDTH // SSD_GROUPS)
    yg = yg * lax.rsqrt(jnp.mean(yg * yg, axis=-1, keepdims=True) + RMS_EPS)
    y_ssd = (yg.reshape(b, L, SSD_WIDTH) * ssd_norm.astype(f32)).astype(h.dtype)
    q = _rms(q.reshape(b, L, ATT_HEADS, ATT_HEAD_DIM), q_norm)
    k = _rms(k.reshape(b, L, ATT_KV_HEADS, ATT_HEAD_DIM), k_norm)
    v = v.reshape(b, L, ATT_KV_HEADS, ATT_HEAD_DIM)
    if kbuf is None:
        o = _swa_prompt(q, k, v, sinks)
        k1, v1 = k[:, -WINDOW:], v[:, -WINDOW:]
    else:
        o, k1, v1 = _swa_cached(q, k, v, kbuf, vbuf, sinks)
    out = jnp.concatenate([y_ssd, o.astype(h.dtype)], axis=-1) @ w_out
    return out, ssm1, conv1, k1, v1


def _rwkv7_mixer(h, wkv0, shift0, mu, w0, w1, w2, a0, a1, a2, g1, g2, kk_coef, ka, rk,
                 wr, wk, wv, wo, ln_w, ln_b):
    f32 = jnp.float32
    b, L, _ = h.shape
    prev = jnp.concatenate([shift0[:, None].astype(h.dtype), h[:, :-1]], axis=1)
    xx = prev - h
    xr, xw, xk, xv, xa, xg = [h + xx * mu[i] for i in range(6)]
    r = xr @ wr
    k = xk @ wk
    v = xv @ wv
    w_log = -jax.nn.softplus(-(w0 + jnp.tanh(xw @ w1) @ w2).astype(f32)) - 0.5
    decay = jnp.exp(-jnp.exp(w_log))
    a = jax.nn.sigmoid((a0 + (xa @ a1) @ a2).astype(f32))
    g = jax.nn.sigmoid(xg @ g1) @ g2
    hs = lambda t: t.astype(f32).reshape(b, L, RWKV_HEADS, RWKV_HEAD)
    kk = hs(k * kk_coef)
    kk = kk / jnp.maximum(jnp.sqrt(jnp.sum(kk * kk, axis=-1, keepdims=True)), 1e-12)
    k = hs(k.astype(f32) * (1.0 + (a - 1.0) * ka.astype(f32)))
    r, decay, v, a = hs(r), hs(decay), hs(v), hs(a)
    tm = lambda t: jnp.moveaxis(t, 1, 0)

    def step(S, inp):
        r_t, w_t, k_t, v_t, a_t, b_t = inp
        sa = jnp.einsum('bhvk,bhk->bhv', S, a_t)
        S = S * w_t[:, :, None, :] + sa[..., None] * b_t[:, :, None, :] + v_t[..., None] * k_t[:, :, None, :]
        return S, jnp.einsum('bhvk,bhk->bhv', S, r_t)

    S1, y = lax.scan(step, wkv0.astype(f32), (tm(r), tm(decay), tm(k), tm(v), tm(-kk), tm(kk * a)))
    y = jnp.moveaxis(y, 0, 1)
    mean = jnp.mean(y, axis=-1, keepdims=True)
    var = jnp.mean((y - mean) ** 2, axis=-1, keepdims=True)
    y = (y - mean) * lax.rsqrt(var + RWKV_GN_EPS)
    y = y * ln_w.astype(f32).reshape(RWKV_HEADS, RWKV_HEAD) + ln_b.astype(f32).reshape(RWKV_HEADS, RWKV_HEAD)
    y = y + jnp.sum(r * k * rk.astype(f32), axis=-1, keepdims=True) * v
    out = (y.reshape(b, L, D_MODEL) * g.astype(f32)).astype(h.dtype) @ wo
    return out, S1, h[:, -1]


def _moe(h, router, wg, wu, wd):
    f32 = jnp.float32
    logits = jnp.einsum('bld,de->ble', h.astype(f32), router.astype(f32))
    top_v, top_i = lax.top_k(logits, TOP_K)
    gates = jax.nn.softmax(top_v, axis=-1)
    combine = jnp.einsum('blk,blke->ble', gates, jax.nn.one_hot(top_i, N_EXPERTS, dtype=f32))
    y = jnp.zeros(h.shape, f32)
    for e in range(N_EXPERTS):
        y = y + combine[..., e:e + 1] * _swiglu(h, wg[e], wu[e], wd[e]).astype(f32)
    return y.astype(h.dtype)


def _trunk(x, ssm, conv, ck, cv, wkv, shift, W):
    n_ssm, n_conv, n_k, n_v, n_wkv, n_shift = [], [], [], [], [], []
    ie = 0
    io = 0
    for layer in range(DEPTH):
        h = _rms(x, W['norm_mix'][layer])
        if layer % 2 == 0:
            out, s1, c1, k1, v1 = _ssd_swa_mixer(
                h, ssm[ie], conv[ie], None if ck is None else ck[ie], None if cv is None else cv[ie],
                W['w_in'][ie], W['conv_w'][ie], W['conv_b'][ie], W['dt_bias'][ie], W['a_log'][ie],
                W['d_skip'][ie], W['ssd_norm'][ie], W['q_norm'][ie], W['k_norm'][ie],
                W['attn_sinks'][ie], W['w_out'][ie])
            x = x + out
            x = x + _swiglu(_rms(x, W['norm_ffn'][layer]), W['ffn_gate'][ie], W['ffn_up'][ie], W['ffn_down'][ie])
            n_ssm.append(s1); n_conv.append(c1); n_k.append(k1); n_v.append(v1)
            ie += 1
        else:
            out, S1, sh1 = _rwkv7_mixer(
                h, wkv[io], shift[io], W['rwkv_mu'][io], W['rwkv_w0'][io], W['rwkv_w1'][io], W['rwkv_w2'][io],
                W['rwkv_a0'][io], W['rwkv_a1'][io], W['rwkv_a2'][io], W['rwkv_g1'][io], W['rwkv_g2'][io],
                W['rwkv_kk'][io], W['rwkv_ka'][io], W['rwkv_rk'][io], W['rwkv_wr'][io], W['rwkv_wk'][io],
                W['rwkv_wv'][io], W['rwkv_wo'][io], W['rwkv_ln_w'][io], W['rwkv_ln_b'][io])
            x = x + out
            x = x + _moe(_rms(x, W['norm_ffn'][layer]), W['moe_router'][io], W['moe_gate'][io],
                         W['moe_up'][io], W['moe_down'][io])
            n_wkv.append(S1); n_shift.append(sh1)
            io += 1
    return (x, jnp.stack(n_ssm), jnp.stack(n_conv), jnp.stack(n_k), jnp.stack(n_v),
            jnp.stack(n_wkv), jnp.stack(n_shift))


def setup_inputs(seed: int = 0) -> dict:
    key = jax.random.key(seed)
    ks = iter(jax.random.split(key, 64))
    f32 = jnp.float32
    nrm = lambda shape, scale: jax.random.normal(next(ks), shape, f32) * scale
    gain = lambda shape: 1.0 + nrm(shape, 0.02)
    unif = lambda shape, lo, hi: jax.random.uniform(next(ks), shape, f32, lo, hi)
    E, O = N_EVEN, N_ODD
    wc = min(WINDOW, PAST_LEN)
    dt0 = jnp.exp(unif((E, SSD_HEADS), math.log(1e-3), math.log(1e-1)))
    dt_bias = dt0 + jnp.log(-jnp.expm1(-dt0))
    return {
        'x_prompt': nrm((BATCH, SEQ, D_MODEL), 1.0),
        'x_sample': nrm((DEC_BATCH, DEC_SEQ, D_MODEL), 1.0),
        'state_ssm': nrm((E, DEC_BATCH, SSD_HEADS, SSD_HEAD_DIM, SSD_STATE), 0.1),
        'state_conv': nrm((E, DEC_BATCH, SSD_CONV - 1, CONV_DIM), 1.0),
        'cache_swa_k': nrm((E, DEC_BATCH, wc, ATT_KV_HEADS, ATT_HEAD_DIM), 1.0),
        'cache_swa_v': nrm((E, DEC_BATCH, wc, ATT_KV_HEADS, ATT_HEAD_DIM), 1.0),
        'state_wkv': nrm((O, DEC_BATCH, RWKV_HEADS, RWKV_HEAD, RWKV_HEAD), 0.1),
        'state_shift': nrm((O, DEC_BATCH, D_MODEL), 1.0),
        'norm_mix': gain((DEPTH, D_MODEL)),
        'norm_ffn': gain((DEPTH, D_MODEL)),
        'w_in': nrm((E, D_MODEL, IN_DIM), D_MODEL ** -0.5),
        'conv_w': nrm((E, SSD_CONV, CONV_DIM), SSD_CONV ** -0.5),
        'conv_b': nrm((E, CONV_DIM), 0.02),
        'dt_bias': dt_bias,
        'a_log': jnp.log(unif((E, SSD_HEADS), 1.0, 16.0)),
        'd_skip': gain((E, SSD_HEADS)),
        'ssd_norm': gain((E, SSD_WIDTH)),
        'q_norm': gain((E, ATT_HEAD_DIM)),
        'k_norm': gain((E, ATT_HEAD_DIM)),
        'attn_sinks': nrm((E, ATT_HEADS), 0.5),
        'w_out': nrm((E, MIX_DIM, D_MODEL), MIX_DIM ** -0.5),
        'ffn_gate': nrm((E, D_MODEL, D_FF), D_MODEL ** -0.5),
        'ffn_up': nrm((E, D_MODEL, D_FF), D_MODEL ** -0.5),
        'ffn_down': nrm((E, D_FF, D_MODEL), D_FF ** -0.5),
        'rwkv_mu': unif((O, 6, D_MODEL), 0.0, 1.0),
        'rwkv_w0': unif((O, D_MODEL), -6.0, -1.0),
        'rwkv_w1': nrm((O, D_MODEL, DECAY_LORA), 0.1 * D_MODEL ** -0.5),
        'rwkv_w2': nrm((O, DECAY_LORA, D_MODEL), 0.1 * DECAY_LORA ** -0.5),
        'rwkv_a0': nrm((O, D_MODEL), 0.1),
        'rwkv_a1': nrm((O, D_MODEL, ICLR_LORA), 0.1 * D_MODEL ** -0.5),
        'rwkv_a2': nrm((O, ICLR_LORA, D_MODEL), 0.1 * ICLR_LORA ** -0.5),
        'rwkv_g1': nrm((O, D_MODEL, GATE_LORA), D_MODEL ** -0.5),
        'rwkv_g2': nrm((O, GATE_LORA, D_MODEL), GATE_LORA ** -0.5),
        'rwkv_kk': 0.85 + nrm((O, D_MODEL), 0.02),
        'rwkv_ka': gain((O, D_MODEL)),
        'rwkv_rk': nrm((O, RWKV_HEADS, RWKV_HEAD), 0.1),
        'rwkv_wr': nrm((O, D_MODEL, D_MODEL), D_MODEL ** -0.5),
        'rwkv_wk': nrm((O, D_MODEL, D_MODEL), D_MODEL ** -0.5),
        'rwkv_wv': nrm((O, D_MODEL, D_MODEL), D_MODEL ** -0.5),
        'rwkv_wo': nrm((O, D_MODEL, D_MODEL), D_MODEL ** -0.5),
        'rwkv_ln_w': gain((O, D_MODEL)),
        'rwkv_ln_b': nrm((O, D_MODEL), 0.02),
        'moe_router': nrm((O, D_MODEL, N_EXPERTS), D_MODEL ** -0.5),
        'moe_gate': nrm((O, N_EXPERTS, D_MODEL, EXPERT_FF), D_MODEL ** -0.5),
        'moe_up': nrm((O, N_EXPERTS, D_MODEL, EXPERT_FF), D_MODEL ** -0.5),
        'moe_down': nrm((O, N_EXPERTS, EXPERT_FF, D_MODEL), EXPERT_FF ** -0.5),
    }


def reference(x_prompt, x_sample, state_ssm, state_conv, cache_swa_k, cache_swa_v, state_wkv, state_shift,
              norm_mix, norm_ffn, w_in, conv_w, conv_b, dt_bias, a_log, d_skip, ssd_norm, q_norm, k_norm,
              attn_sinks, w_out, ffn_gate, ffn_up, ffn_down, rwkv_mu, rwkv_w0, rwkv_w1, rwkv_w2, rwkv_a0,
              rwkv_a1, rwkv_a2, rwkv_g1, rwkv_g2, rwkv_kk, rwkv_ka, rwkv_rk, rwkv_wr, rwkv_wk, rwkv_wv, rwkv_wo,
              rwkv_ln_w, rwkv_ln_b, moe_router, moe_gate, moe_up, moe_down):
    W = dict(norm_mix=norm_mix, norm_ffn=norm_ffn, w_in=w_in, conv_w=conv_w, conv_b=conv_b, dt_bias=dt_bias,
             a_log=a_log, d_skip=d_skip, ssd_norm=ssd_norm, q_norm=q_norm, k_norm=k_norm, attn_sinks=attn_sinks,
             w_out=w_out, ffn_gate=ffn_gate, ffn_up=ffn_up, ffn_down=ffn_down, rwkv_mu=rwkv_mu, rwkv_w0=rwkv_w0,
             rwkv_w1=rwkv_w1, rwkv_w2=rwkv_w2, rwkv_a0=rwkv_a0, rwkv_a1=rwkv_a1, rwkv_a2=rwkv_a2, rwkv_g1=rwkv_g1,
             rwkv_g2=rwkv_g2, rwkv_kk=rwkv_kk, rwkv_ka=rwkv_ka, rwkv_rk=rwkv_rk, rwkv_wr=rwkv_wr, rwkv_wk=rwkv_wk,
             rwkv_wv=rwkv_wv, rwkv_wo=rwkv_wo, rwkv_ln_w=rwkv_ln_w, rwkv_ln_b=rwkv_ln_b, moe_router=moe_router,
             moe_gate=moe_gate, moe_up=moe_up, moe_down=moe_down)
    bp = x_prompt.shape[0]
    z_ssm = jnp.zeros((N_EVEN, bp) + state_ssm.shape[2:], jnp.float32)
    z_conv = jnp.zeros((N_EVEN, bp) + state_conv.shape[2:], x_prompt.dtype)
    z_wkv = jnp.zeros((N_ODD, bp) + state_wkv.shape[2:], jnp.float32)
    z_shift = jnp.zeros((N_ODD, bp) + state_shift.shape[2:], x_prompt.dtype)
    y_prompt, p_ssm, p_conv, p_k, p_v, p_wkv, p_shift = _trunk(
        x_prompt, z_ssm, z_conv, None, None, z_wkv, z_shift, W)
    y_sample, s_ssm, s_conv, s_k, s_v, s_wkv, s_shift = _trunk(
        x_sample, state_ssm, state_conv, cache_swa_k, cache_swa_v, state_wkv, state_shift, W)
    return (y_prompt, y_sample, p_ssm, p_conv, p_k, p_v, p_wkv, p_shift,
            s_ssm, s_conv, s_k, s_v, s_wkv, s_shift)
```

```python
import functools

import jax
import jax.numpy as jnp
from jax import lax
from jax.experimental import pallas as pl
from jax.experimental.pallas import tpu as pltpu

F32 = jnp.float32
BF16 = jnp.bfloat16

D_MODEL = 1024
SSD_HEAD_DIM = 64
SSD_HEADS = 16
SSD_GROUPS = 2
SSD_STATE = 128
SSD_CONV = 4
SSD_CHUNK = 128
SSD_WIDTH = 1024
CONV_DIM = SSD_WIDTH + 2 * SSD_GROUPS * SSD_STATE
ATT_HEAD_DIM = 64
ATT_HEADS = 16
ATT_KV_HEADS = 4
ATT_REP = ATT_HEADS // ATT_KV_HEADS
WINDOW = 128
Q_DIM = ATT_HEADS * ATT_HEAD_DIM
KV_DIM = ATT_KV_HEADS * ATT_HEAD_DIM
RWKV_HEAD = 64
RWKV_HEADS = 16
RWKV_GN_EPS = 64e-5
RWKV_CHUNK = 64
D_FF = 2816
N_EXPERTS = 8
RMS_EPS = 1e-6
NEG_INF = -1e30

LANES = 128
MXU_DIM = 256
VMEM_LIMIT_BYTES = 56 * 1024 * 1024
TOKEN_TILE = 512
FF_CHUNK = MXU_DIM
HIGHEST = lax.Precision.HIGHEST


def _dot(a, b, precision=None):
    return jnp.dot(a, b, preferred_element_type=F32, precision=precision)


def _dot_nt(a, b, precision=None):
    return lax.dot_general(a, b, (((1,), (1,)), ((), ())), preferred_element_type=F32, precision=precision)


def _dot_tn(a, b, precision=None):
    return lax.dot_general(a, b, (((0,), (0,)), ((), ())), preferred_element_type=F32, precision=precision)


def _rms_rows(x, g):
    return x * lax.rsqrt(jnp.mean(x * x, axis=-1, keepdims=True) + RMS_EPS) * g


def _sigmoid(x):
    return 1.0 / (1.0 + jnp.exp(-x))


def _silu(x):
    return x * _sigmoid(x)


def _softplus(x):
    return jnp.maximum(x, 0.0) + jnp.log(1.0 + jnp.exp(-jnp.abs(x)))


def _const_spec(shape):
    zeros = (0,) * len(shape)
    return pl.BlockSpec(shape, lambda *_: zeros, pipeline_mode=pl.Buffered(1))


def _params(*semantics):
    return pltpu.CompilerParams(dimension_semantics=semantics, vmem_limit_bytes=VMEM_LIMIT_BYTES)


def _head_sums(t, ones_blk):
    return _dot(t.astype(BF16), ones_blk)


def _block_ones():
    r = jnp.arange(MXU_DIM) // ATT_HEAD_DIM
    return (r[:, None] == r[None, :]).astype(BF16)


def _inproj_kernel(x_ref, g_ref, wz_ref, wxbc_ref, wdt_ref, wq_ref, wk_ref, wv_ref, qn_ref, kn_ref, ones_ref,
                   z_ref, xbc_ref, dt_ref, q_ref, k_ref, v_ref):
    h = _rms_rows(x_ref[...], g_ref[...]).astype(BF16)
    ones_blk = ones_ref[...]

    def head_rms(t, gain):
        outs = []
        for c in range(t.shape[1] // MXU_DIM):
            tc = t[:, c * MXU_DIM:(c + 1) * MXU_DIM]
            ms = _head_sums(tc * tc, ones_blk) * (1.0 / ATT_HEAD_DIM)
            outs.append(tc * lax.rsqrt(ms + RMS_EPS))
        return jnp.concatenate(outs, axis=1) * gain

    z_ref[...] = _dot(h, wz_ref[...]).astype(z_ref.dtype)
    xbc_ref[...] = _dot(h, wxbc_ref[...])
    dt_ref[...] = _dot(h, wdt_ref[...])
    q_ref[...] = head_rms(_dot(h, wq_ref[...]), qn_ref[...]).astype(q_ref.dtype)
    k_ref[...] = head_rms(_dot(h, wk_ref[...]), kn_ref[...])
    v_ref[...] = _dot(h, wv_ref[...])


def _inproj(x, gain, wz, wxbc, wdt, wq, wk, wv, qn, kn, ones_blk):
    t = x.shape[0]
    tm = min(TOKEN_TILE, t)
    row = lambda n: pl.BlockSpec((tm, n), lambda i: (i, 0))
    return pl.pallas_call(
        _inproj_kernel,
        grid=(t // tm,),
        in_specs=[row(D_MODEL), _const_spec((1, D_MODEL)),
                  _const_spec(wz.shape), _const_spec(wxbc.shape), _const_spec(wdt.shape),
                  _const_spec(wq.shape), _const_spec(wk.shape), _const_spec(wv.shape),
                  _const_spec((1, Q_DIM)), _const_spec((1, KV_DIM)), _const_spec((MXU_DIM, MXU_DIM))],
        out_specs=[row(SSD_WIDTH), row(CONV_DIM), row(LANES), row(Q_DIM), row(KV_DIM), row(KV_DIM)],
        out_shape=[jax.ShapeDtypeStruct((t, SSD_WIDTH), BF16), jax.ShapeDtypeStruct((t, CONV_DIM), F32),
                   jax.ShapeDtypeStruct((t, LANES), F32), jax.ShapeDtypeStruct((t, Q_DIM), BF16),
                   jax.ShapeDtypeStruct((t, KV_DIM), F32), jax.ShapeDtypeStruct((t, KV_DIM), F32)],
        compiler_params=_params("arbitrary"),
        name="inproj",
    )(x, gain, wz, wxbc, wdt, wq, wk, wv, qn, kn, ones_blk)


def _ssd_kernel(xbc_ref, dt_ref, z_ref, conv0_ref, ssm0_ref, cw_ref, cb_ref, dtb_ref, alog_ref, dsk_ref, nrm_ref,
                y_ref, ssm1_ref, conv1_ref, xpad_ref, dtpad_ref, s_ref, *, q, lin, lv, nchunks):
    c = pl.program_id(1)
    n_pairs = SSD_HEADS // 2
    hist = SSD_CONV - 1
    base = 8 - hist

    @pl.when(c == 0)
    def _():
        xpad_ref[...] = jnp.zeros(xpad_ref.shape, F32)
        dtpad_ref[...] = jnp.zeros(dtpad_ref.shape, F32)
        xpad_ref[base:8, :] = conv0_ref[0]
        for jb in range(n_pairs):
            s_ref[:, jb * LANES:(jb + 1) * LANES] = ssm0_ref[0, jb * LANES:(jb + 1) * LANES, :].T

    xpad_ref[8:8 + lin, :] = xbc_ref[0]
    dtpad_ref[0:lin, :] = dt_ref[0]

    conv = cb_ref[...]
    for j in range(SSD_CONV):
        conv = conv + xpad_ref[base + j:base + j + q, :] * cw_ref[j:j + 1, :]
    tail = xpad_ref[base + lv:8 + lv, :]
    conv1_ref[0] = tail
    xpad_ref[base:8, :] = tail

    xc = _silu(conv)
    xs = xc[:, :SSD_WIDTH]
    bm = xc[:, SSD_WIDTH:SSD_WIDTH + SSD_GROUPS * SSD_STATE].astype(BF16)
    cm = xc[:, SSD_WIDTH + SSD_GROUPS * SSD_STATE:].astype(BF16)

    row_i = lax.broadcasted_iota(jnp.int32, (q, LANES), 0)
    col_i = lax.broadcasted_iota(jnp.int32, (q, LANES), 1)
    first_half = col_i < SSD_HEAD_DIM
    dt = jnp.where(row_i < lv, _softplus(dtpad_ref[...] + dtb_ref[...]), 0.0)
    a = dt * (-jnp.exp(alog_ref[...]))
    r_q = lax.broadcasted_iota(jnp.int32, (q, q), 0)
    c_q = lax.broadcasted_iota(jnp.int32, (q, q), 1)
    causal = r_q >= c_q
    a_cs = _dot(causal.astype(F32), a, precision=HIGHEST)
    a_cs_t = a_cs.T
    ea = jnp.exp(a_cs)
    dte = jnp.exp(a_cs[q - 1:q, :] - a_cs)

    def colb(arr, h):
        return jnp.broadcast_to(arr[:, h:h + 1], (q, LANES))

    def pairb(arr, h0):
        return jnp.where(first_half, colb(arr, h0), colb(arr, h0 + 1))

    cbs = []
    for g in range(SSD_GROUPS):
        sl = slice(g * SSD_STATE, (g + 1) * SSD_STATE)
        cbs.append(_dot_nt(cm[:, sl], bm[:, sl]))

    ys = []
    for j in range(n_pairs):
        g = (2 * j) // (SSD_HEADS // SSD_GROUPS)
        gsl = slice(g * SSD_STATE, (g + 1) * SSD_STATE)
        psl = slice(j * LANES, (j + 1) * LANES)
        xs_p = xs[:, psl]
        xdt = xs_p * pairb(dt, 2 * j)
        xdt_b = xdt.astype(BF16)
        yd = []
        for hh in (2 * j, 2 * j + 1):
            diff = colb(a_cs, hh) - a_cs_t[hh:hh + 1, :]
            lmat = jnp.exp(jnp.where(causal, diff, NEG_INF))
            yd.append(_dot((cbs[g] * lmat).astype(BF16), xdt_b))
        y_diag = jnp.where(first_half, yd[0], yd[1])
        ea_p = pairb(ea, 2 * j)
        s_old = s_ref[:, psl]
        y_off = _dot(cm[:, gsl], s_old.astype(BF16)) * ea_p
        s_ref[:, psl] = s_old * ea_p[q - 1:q, :] + _dot_tn(bm[:, gsl], (xdt * pairb(dte, 2 * j)).astype(BF16))
        ys.append(y_diag + y_off + xs_p * dsk_ref[:, psl])
    y = jnp.concatenate(ys, axis=1)[0:lin]

    y = y * _silu(z_ref[0].astype(F32))
    gw = SSD_WIDTH // SSD_GROUPS
    outs = []
    for g in range(SSD_GROUPS):
        yg = y[:, g * gw:(g + 1) * gw]
        outs.append(yg * lax.rsqrt(jnp.mean(yg * yg, axis=-1, keepdims=True) + RMS_EPS))
    y_ref[0] = (jnp.concatenate(outs, axis=1) * nrm_ref[...]).astype(y_ref.dtype)

    @pl.when(c == nchunks - 1)
    def _():
        for jb in range(n_pairs):
            ssm1_ref[0, jb * LANES:(jb + 1) * LANES, :] = s_ref[:, jb * LANES:(jb + 1) * LANES].T


def _ssd(xbc, dt, z, conv0, ssm0, cw, cb, dtb, alog, dsk, nrm, *, lv):
    b, lp, _ = xbc.shape
    q = SSD_CHUNK
    lin = min(q, lp)
    nchunks = lp // lin
    lv_chunk = min(lv, lin)
    kern = functools.partial(_ssd_kernel, q=q, lin=lin, lv=lv_chunk, nchunks=nchunks)
    seq = lambda n: pl.BlockSpec((1, lin, n), lambda i, c: (i, c, 0))
    per_b = lambda s: pl.BlockSpec((1,) + s, lambda i, c: (i, 0, 0))
    return pl.pallas_call(
        kern,
        grid=(b, nchunks),
        in_specs=[seq(CONV_DIM), seq(LANES), seq(SSD_WIDTH),
                  per_b((SSD_CONV - 1, CONV_DIM)), per_b((SSD_WIDTH, SSD_STATE)),
                  _const_spec((SSD_CONV, CONV_DIM)), _const_spec((1, CONV_DIM)), _const_spec((1, LANES)),
                  _const_spec((1, LANES)), _const_spec((1, SSD_WIDTH)), _const_spec((1, SSD_WIDTH))],
        out_specs=[seq(SSD_WIDTH), per_b((SSD_WIDTH, SSD_STATE)), per_b((SSD_CONV - 1, CONV_DIM))],
        out_shape=[jax.ShapeDtypeStruct((b, lp, SSD_WIDTH), BF16),
                   jax.ShapeDtypeStruct((b, SSD_WIDTH, SSD_STATE), F32),
                   jax.ShapeDtypeStruct((b, SSD_CONV - 1, CONV_DIM), F32)],
        scratch_shapes=[pltpu.VMEM((q + 8, CONV_DIM), F32), pltpu.VMEM((q, LANES), F32),
                        pltpu.VMEM((SSD_STATE, SSD_WIDTH), F32)],
        compiler_params=_params("arbitrary", "arbitrary"),
        name="ssd",
    )(xbc, dt, z, conv0, ssm0, cw, cb, dtb, alog, dsk, nrm)


def _sink_softmax_pv(s, allowed, sink_col, v_b):
    s = jnp.where(allowed, s, NEG_INF)
    m = jnp.maximum(jnp.max(s, axis=-1, keepdims=True), sink_col)
    p = jnp.exp(s - m)
    denom = jnp.sum(p, axis=-1, keepdims=True) + jnp.exp(sink_col - m)
    return _dot(p.astype(BF16), v_b) / denom


def _swa_prompt_kernel(sink_ref, q_ref, kp_ref, kc_ref, vp_ref, vc_ref, o_ref):
    i = pl.program_id(1)
    w = WINDOW
    d = ATT_HEAD_DIM
    qb = q_ref[0]
    kk = jnp.concatenate([kp_ref[0], kc_ref[0]], axis=0).astype(BF16)
    vv = jnp.concatenate([vp_ref[0], vc_ref[0]], axis=0).astype(BF16)
    rows = lax.broadcasted_iota(jnp.int32, (ATT_REP * w, 2 * w), 0) % w
    cols = lax.broadcasted_iota(jnp.int32, (ATT_REP * w, 2 * w), 1)
    rel = rows + w - cols
    allowed = (rel >= 0) & (rel <= w) & ((cols >= w) | (i > 0))
    outs = []
    for j in range(ATT_KV_HEADS):
        qj = jnp.concatenate([qb[:, (ATT_REP * j + r) * d:(ATT_REP * j + r + 1) * d] for r in range(ATT_REP)], axis=0)
        sink_col = jnp.concatenate(
            [jnp.full((w, 1), sink_ref[ATT_REP * j + r], F32) for r in range(ATT_REP)], axis=0)
        s = _dot_nt(qj, kk[:, j * d:(j + 1) * d]) * (d ** -0.5)
        o = _sink_softmax_pv(s, allowed, sink_col, vv[:, j * d:(j + 1) * d])
        outs.extend(o[r * w:(r + 1) * w] for r in range(ATT_REP))
    o_ref[0] = jnp.concatenate(outs, axis=1).astype(o_ref.dtype)


def _swa_prompt(q, k, v, sinks):
    b, l, _ = q.shape
    nb = l // WINDOW
    cur = lambda n: pl.BlockSpec((1, WINDOW, n), lambda i, c: (i, c, 0))
    prev = lambda n: pl.BlockSpec((1, WINDOW, n), lambda i, c: (i, jnp.maximum(c - 1, 0), 0))
    return pl.pallas_call(
        _swa_prompt_kernel,
        grid=(b, nb),
        in_specs=[pl.BlockSpec(memory_space=pltpu.SMEM),
                  cur(Q_DIM), prev(KV_DIM), cur(KV_DIM), prev(KV_DIM), cur(KV_DIM)],
        out_specs=cur(Q_DIM),
        out_shape=jax.ShapeDtypeStruct((b, l, Q_DIM), BF16),
        compiler_params=_params("arbitrary", "arbitrary"),
        name="swa_prompt",
    )(sinks, q, k, k, v, v)


def _swa_cached_kernel(sink_ref, q_ref, k_ref, v_ref, kbuf_ref, vbuf_ref, o_ref, k1_ref, v1_ref, *, nbatch, l):
    wc = kbuf_ref.shape[1]
    d = ATT_HEAD_DIM
    pad = 8 - l
    nk = wc + 8
    rows = lax.broadcasted_iota(jnp.int32, (ATT_REP * l, nk), 0) % l
    cols = lax.broadcasted_iota(jnp.int32, (ATT_REP * l, nk), 1)
    rel = rows + wc - cols
    allowed = (rel >= 0) & (rel <= WINDOW)
    zpad = jnp.zeros((pad, KV_DIM), F32)
    for bi in range(nbatch):
        qb = q_ref[bi * l:(bi + 1) * l, :]
        kn = k_ref[bi * l:(bi + 1) * l, :]
        vn = v_ref[bi * l:(bi + 1) * l, :]
        kc = jnp.concatenate([kbuf_ref[bi], kn, zpad], axis=0).astype(BF16)
        vc = jnp.concatenate([vbuf_ref[bi], vn, zpad], axis=0).astype(BF16)
        outs = []
        for j in range(ATT_KV_HEADS):
            qj = jnp.concatenate(
                [qb[:, (ATT_REP * j + r) * d:(ATT_REP * j + r + 1) * d] for r in range(ATT_REP)], axis=0)
            sink_col = jnp.concatenate(
                [jnp.full((l, 1), sink_ref[ATT_REP * j + r], F32) for r in range(ATT_REP)], axis=0)
            s = _dot_nt(qj, kc[:, j * d:(j + 1) * d]) * (d ** -0.5)
            o = _sink_softmax_pv(s, allowed, sink_col, vc[:, j * d:(j + 1) * d])
            outs.extend(o[r * l:(r + 1) * l] for r in range(ATT_REP))
        o_ref[bi * l:(bi + 1) * l, :] = jnp.concatenate(outs, axis=1).astype(o_ref.dtype)
        k1_ref[bi, 0:wc - l, :] = kbuf_ref[bi, l:wc, :]
        k1_ref[bi, wc - l:wc, :] = kn
        v1_ref[bi, 0:wc - l, :] = vbuf_ref[bi, l:wc, :]
        v1_ref[bi, wc - l:wc, :] = vn


def _swa_cached(q, k, v, kbuf, vbuf, sinks, *, l):
    b, wc, _ = kbuf.shape
    nbatch = 8
    kern = functools.partial(_swa_cached_kernel, nbatch=nbatch, l=l)
    rows = lambda n: pl.BlockSpec((nbatch * l, n), lambda i: (i, 0))
    cache = pl.BlockSpec((nbatch, wc, KV_DIM), lambda i: (i, 0, 0))
    return pl.pallas_call(
        kern,
        grid=(b // nbatch,),
        in_specs=[pl.BlockSpec(memory_space=pltpu.SMEM), rows(Q_DIM), rows(KV_DIM), rows(KV_DIM), cache, cache],
        out_specs=[rows(Q_DIM), cache, cache],
        out_shape=[jax.ShapeDtypeStruct((b * l, Q_DIM), BF16),
                   jax.ShapeDtypeStruct((b, wc, KV_DIM), F32), jax.ShapeDtypeStruct((b, wc, KV_DIM), F32)],
        compiler_params=_params("arbitrary"),
        name="swa_cached",
    )(sinks, q, k, v, kbuf, vbuf)


def _swiglu_act(h, wg_ref, wu_ref, act_ref):
    for c in range(D_FF // FF_CHUNK):
        sl = slice(c * FF_CHUNK, (c + 1) * FF_CHUNK)
        act_ref[:, sl] = (_silu(_dot(h, wg_ref[:, sl])) * _dot(h, wu_ref[:, sl])).astype(BF16)


def _outproj_ffn_kernel(x_ref, ya_ref, yb_ref, woa_ref, wob_ref, g_ref, wg_ref, wu_ref, wd_ref, o_ref, act_ref):
    x1 = x_ref[...] + _dot(ya_ref[...], woa_ref[...]) + _dot(yb_ref[...], wob_ref[...])
    h = _rms_rows(x1, g_ref[...]).astype(BF16)
    _swiglu_act(h, wg_ref, wu_ref, act_ref)
    o_ref[...] = x1 + _dot(act_ref[...], wd_ref[...])


def _outproj_ffn(x, ya, yb, woa, wob, gain, wg, wu, wd):
    t = x.shape[0]
    tm = min(TOKEN_TILE, t)
    row = lambda n: pl.BlockSpec((tm, n), lambda i: (i, 0))
    return pl.pallas_call(
        _outproj_ffn_kernel,
        grid=(t // tm,),
        in_specs=[row(D_MODEL), row(SSD_WIDTH), row(Q_DIM), _const_spec(woa.shape), _const_spec(wob.shape),
                  _const_spec((1, D_MODEL)), _const_spec(wg.shape), _const_spec(wu.shape), _const_spec(wd.shape)],
        out_specs=row(D_MODEL),
        out_shape=jax.ShapeDtypeStruct((t, D_MODEL), F32),
        scratch_shapes=[pltpu.VMEM((tm, D_FF), BF16)],
        compiler_params=_params("arbitrary"),
        name="outproj_ffn",
    )(x, ya, yb, woa, wob, gain, wg, wu, wd)


def _rwkv_prep_kernel(x_ref, first_ref, g_ref, mu_ref, w0_ref, a0_ref, kkc_ref, ka_ref,
                      wr_ref, wk_ref, wv_ref, w1_ref, w2_ref, a1_ref, a2_ref, g1_ref, g2_ref, ones_ref,
                      r_ref, lw_ref, k_ref, v_ref, an_ref, b_ref, gate_ref, h_ref, hs_ref, *, tm, seq_len):
    i = pl.program_id(0)
    h = _rms_rows(x_ref[...], g_ref[...])
    h_ref[...] = h
    tiles_per_seq = max(seq_len // tm, 1)

    if seq_len >= tm:
        @pl.when(i % tiles_per_seq == 0)
        def _():
            hs_ref[7:8, :] = first_ref[0]
    hs_ref[8:8 + tm, :] = h
    prev = hs_ref[7:7 + tm, :]
    if seq_len < tm:
        rows = lax.broadcasted_iota(jnp.int32, (tm, D_MODEL), 0)
        prev = jnp.where(rows % seq_len == 0, first_ref[...], prev)
    else:
        hs_ref[7:8, :] = h[tm - 1:tm, :]
    xx = prev - h
    mix = lambda n: (h + xx * mu_ref[n:n + 1, :]).astype(BF16)
    r = _dot(mix(0), wr_ref[...])
    k = _dot(mix(2), wk_ref[...])
    v = _dot(mix(3), wv_ref[...])
    w_lora = _dot(jnp.tanh(_dot(mix(1), w1_ref[...])).astype(BF16), w2_ref[...])
    a_lora = _dot(_dot(mix(4), a1_ref[...]).astype(BF16), a2_ref[...])
    gate = _dot(_sigmoid(_dot(mix(5), g1_ref[...])).astype(BF16), g2_ref[...])
    w_log = -_softplus(-(w0_ref[...] + w_lora)) - 0.5
    a_sig = _sigmoid(a0_ref[...] + a_lora)
    kk = k * kkc_ref[...]
    ones_blk = ones_ref[...]
    parts = []
    for c in range(D_MODEL // MXU_DIM):
        kc = kk[:, c * MXU_DIM:(c + 1) * MXU_DIM]
        nrm = jnp.maximum(jnp.sqrt(_head_sums(kc * kc, ones_blk)), 1e-12)
        parts.append(kc / nrm)
    kk = jnp.concatenate(parts, axis=1)
    r_ref[...] = r
    lw_ref[...] = -jnp.exp(w_log)
    k_ref[...] = k * (1.0 + (a_sig - 1.0) * ka_ref[...])
    v_ref[...] = v
    an_ref[...] = -kk
    b_ref[...] = kk * a_sig
    gate_ref[...] = gate.astype(gate_ref.dtype)


def _rwkv_prep(x, first, gain, mu, w0, a0, kkc, ka, wr, wk, wv, w1, w2, a1, a2, g1, g2, ones_blk, *, seq_len):
    t = x.shape[0]
    tm = min(TOKEN_TILE, t)
    kern = functools.partial(_rwkv_prep_kernel, tm=tm, seq_len=seq_len)
    row = lambda n: pl.BlockSpec((tm, n), lambda i: (i, 0))
    if seq_len >= tm:
        tiles_per_seq = seq_len // tm
        first_spec = pl.BlockSpec((1, 1, D_MODEL), lambda i: (i // tiles_per_seq, 0, 0))
    else:
        first_spec = row(D_MODEL)
    vec = _const_spec((1, D_MODEL))
    outs = [jax.ShapeDtypeStruct((t, D_MODEL), F32)] * 6 + [
        jax.ShapeDtypeStruct((t, D_MODEL), BF16), jax.ShapeDtypeStruct((t, D_MODEL), F32)]
    return pl.pallas_call(
        kern,
        grid=(t // tm,),
        in_specs=[row(D_MODEL), first_spec, vec, _const_spec((6, D_MODEL)), vec, vec, vec, vec,
                  _const_spec(wr.shape), _const_spec(wk.shape), _const_spec(wv.shape),
                  _const_spec(w1.shape), _const_spec(w2.shape), _const_spec(a1.shape), _const_spec(a2.shape),
                  _const_spec(g1.shape), _const_spec(g2.shape), _const_spec((MXU_DIM, MXU_DIM))],
        out_specs=[row(D_MODEL)] * 8,
        out_shape=outs,
        scratch_shapes=[pltpu.VMEM((tm + 8, D_MODEL), F32)],
        compiler_params=_params("arbitrary"),
        name="rwkv_prep",
    )(x, first, gain, mu, w0, a0, kkc, ka, wr, wk, wv, w1, w2, a1, a2, g1, g2, ones_blk)


def _rwkv_scan_kernel(r_ref, lw_ref, k_ref, v_ref, a_ref, b_ref, s0_ref, y_ref, s1_ref, s_ref, *, nsub, nsteps):
    step = pl.program_id(1)
    c = RWKV_CHUNK
    n_pairs = RWKV_HEADS // 2
    two_c = 2 * c

    @pl.when(step == 0)
    def _():
        s_ref[...] = s0_ref[0]

    lane = lax.broadcasted_iota(jnp.int32, (c, LANES), 1)
    m_a = lane < RWKV_HEAD
    r_i = lax.broadcasted_iota(jnp.int32, (two_c, two_c), 0)
    c_i = lax.broadcasted_iota(jnp.int32, (two_c, two_c), 1)
    strict = r_i > c_i
    incl = r_i >= c_i
    blk = (r_i < c) == (c_i < c)
    t_r = lax.broadcasted_iota(jnp.int32, (c, c), 0)
    t_c = lax.broadcasted_iota(jnp.int32, (c, c), 1)
    tri = (t_r >= t_c).astype(F32)

    def stack(x):
        return jnp.concatenate([jnp.where(m_a, x, 0.0), jnp.where(m_a, 0.0, x)], axis=0)

    def sub_chunk(ci, carry):
        rows = pl.ds(pl.multiple_of(ci * c, c), c)
        for p in range(n_pairs):
            psl = slice(p * LANES, (p + 1) * LANES)
            lw = lw_ref[0, rows, psl]
            cl = _dot(tri, lw, precision=HIGHEST)
            e_pos = jnp.exp(cl)
            e_neg = jnp.exp(-cl)
            e_prev = jnp.exp(cl - lw)
            cl_end = cl[c - 1:c, :]
            e_end = jnp.exp(cl_end - cl)
            rr = r_ref[0, rows, psl]
            kk = k_ref[0, rows, psl]
            vv = v_ref[0, rows, psl]
            aa = a_ref[0, rows, psl]
            bb = b_ref[0, rows, psl]
            lhs = jnp.concatenate([stack(aa * e_prev), stack(rr * e_pos)], axis=0).astype(BF16)
            rhs = jnp.concatenate([stack(bb * e_neg), stack(kk * e_neg)], axis=0).astype(BF16)
            gmat = _dot_nt(lhs, rhs)
            n_ab = jnp.where(strict, gmat[:two_c, :two_c], 0.0)
            a_ak = jnp.where(strict, gmat[:two_c, two_c:], 0.0)
            a_rb = jnp.where(incl, gmat[two_c:, :two_c], 0.0)
            a_rk = jnp.where(incl, gmat[two_c:, two_c:], 0.0)
            s_old = s_ref[p]
            ph = _dot_nt(lhs, s_old.astype(BF16))
            v_s = stack(vv)
            v_b = v_s.astype(BF16)
            u = ph[:two_c] + _dot(a_ak.astype(BF16), v_b)
            npow = n_ab
            for lvl in range(6):
                nb = npow.astype(BF16)
                u = u + _dot(nb, u.astype(BF16))
                if lvl < 5:
                    npow = _dot(nb, nb)
            uv = jnp.concatenate([u, v_s], axis=0).astype(BF16)
            y_s = ph[two_c:] + _dot(jnp.concatenate([a_rb, a_rk], axis=1).astype(BF16), uv)
            y_ref[0, rows, psl] = y_s[:c] + y_s[c:]
            bk = jnp.concatenate([stack(bb * e_end), stack(kk * e_end)], axis=0).astype(BF16)
            s_new = s_old * jnp.exp(cl_end) + _dot_tn(uv, bk)
            s_ref[p] = jnp.where(blk, s_new, 0.0)
        return carry

    lax.fori_loop(0, nsub, sub_chunk, 0)

    @pl.when(step == nsteps - 1)
    def _():
        s1_ref[0] = s_ref[...]


def _rwkv_scan(r, lw, k, v, an, b, s0):
    bsz, lp, _ = r.shape
    rows = min(lp, 512)
    nsteps = lp // rows
    nsub = rows // RWKV_CHUNK
    kern = functools.partial(_rwkv_scan_kernel, nsub=nsub, nsteps=nsteps)
    seq = pl.BlockSpec((1, rows, D_MODEL), lambda i, s: (i, s, 0))
    st = pl.BlockSpec((1, RWKV_HEADS // 2, LANES, LANES), lambda i, s: (i, 0, 0, 0))
    return pl.pallas_call(
        kern,
        grid=(bsz, nsteps),
        in_specs=[seq] * 6 + [st],
        out_specs=[seq, st],
        out_shape=[jax.ShapeDtypeStruct((bsz, lp, D_MODEL), F32),
                   jax.ShapeDtypeStruct((bsz, RWKV_HEADS // 2, LANES, LANES), F32)],
        scratch_shapes=[pltpu.VMEM((RWKV_HEADS // 2, LANES, LANES), F32)],
        compiler_params=_params("arbitrary", "arbitrary"),
        name="rwkv_scan",
    )(r, lw, k, v, an, b, s0)


def _rwkv_out_kernel(x_ref, y_ref, r_ref, k_ref, v_ref, gate_ref, lnw_ref, lnb_ref, rk_ref, wo_ref, ones_ref, o_ref):
    ones_blk = ones_ref[...]
    inv = 1.0 / RWKV_HEAD
    parts = []
    for c in range(D_MODEL // MXU_DIM):
        sl = slice(c * MXU_DIM, (c + 1) * MXU_DIM)
        y = y_ref[:, sl]
        yh = y.astype(BF16)
        mean = (_dot(yh, ones_blk) + _head_sums(y - yh.astype(F32), ones_blk)) * inv
        yc = y - mean
        var = _head_sums(yc * yc, ones_blk) * inv
        yn = yc * lax.rsqrt(var + RWKV_GN_EPS) * lnw_ref[:, sl] + lnb_ref[:, sl]
        rk = r_ref[:, sl] * k_ref[:, sl] * rk_ref[:, sl]
        rkh = rk.astype(BF16)
        bonus = _dot(rkh, ones_blk) + _head_sums(rk - rkh.astype(F32), ones_blk)
        parts.append(yn + bonus * v_ref[:, sl])
    out = jnp.concatenate(parts, axis=1) * gate_ref[...].astype(F32)
    o_ref[...] = x_ref[...] + _dot(out.astype(BF16), wo_ref[...])


def _rwkv_out(x, y, r, k, v, gate, lnw, lnb, rk, wo, ones_blk):
    t = x.shape[0]
    tm = min(TOKEN_TILE, t)
    row = pl.BlockSpec((tm, D_MODEL), lambda i: (i, 0))
    vec = _const_spec((1, D_MODEL))
    return pl.pallas_call(
        _rwkv_out_kernel,
        grid=(t // tm,),
        in_specs=[row] * 6 + [vec, vec, vec, _const_spec(wo.shape), _const_spec((MXU_DIM, MXU_DIM))],
        out_specs=row,
        out_shape=jax.ShapeDtypeStruct((t, D_MODEL), F32),
        compiler_params=_params("arbitrary"),
        name="rwkv_out",
    )(x, y, r, k, v, gate, lnw, lnb, rk, wo, ones_blk)


def _moe_kernel(x_ref, g_ref, router_ref, wg_ref, wu_ref, wd_ref, o_ref, acc_ref, h_ref, comb_ref, act_ref):
    e = pl.program_id(1)

    @pl.when(e == 0)
    def _():
        x = x_ref[...]
        acc_ref[...] = x
        h = _rms_rows(x, g_ref[...])
        h_ref[...] = h.astype(BF16)
        logits = _dot(h, router_ref[...], precision=HIGHEST)
        lane = lax.broadcasted_iota(jnp.int32, logits.shape, 1)
        logits = jnp.where(lane < N_EXPERTS, logits, -jnp.inf)
        m1 = jnp.max(logits, axis=-1, keepdims=True)
        i1 = jnp.min(jnp.where(logits == m1, lane, LANES), axis=-1, keepdims=True)
        rest = jnp.where(lane == i1, -jnp.inf, logits)
        m2 = jnp.max(rest, axis=-1, keepdims=True)
        i2 = jnp.min(jnp.where(rest == m2, lane, LANES), axis=-1, keepdims=True)
        e2 = jnp.exp(m2 - m1)
        g1 = 1.0 / (1.0 + e2)
        g2 = e2 / (1.0 + e2)
        comb_ref[...] = jnp.where(lane == i1, g1, 0.0) + jnp.where(lane == i2, g2, 0.0)

    lane = lax.broadcasted_iota(jnp.int32, comb_ref.shape, 1)
    gate_col = jnp.sum(jnp.where(lane == e, comb_ref[...], 0.0), axis=-1, keepdims=True)
    _swiglu_act(h_ref[...], wg_ref.at[0], wu_ref.at[0], act_ref)
    acc_ref[...] += gate_col * _dot(act_ref[...], wd_ref[0])

    @pl.when(e == N_EXPERTS - 1)
    def _():
        o_ref[...] = acc_ref[...]


def _moe(x, gain, router, wg, wu, wd):
    t = x.shape[0]
    tm = min(TOKEN_TILE, t)
    row = pl.BlockSpec((tm, D_MODEL), lambda i, e: (i, 0))
    return pl.pallas_call(
        _moe_kernel,
        grid=(t // tm, N_EXPERTS),
        in_specs=[row, _const_spec((1, D_MODEL)), _const_spec((D_MODEL, LANES)),
                  pl.BlockSpec((1, D_MODEL, D_FF), lambda i, e: (e, 0, 0)),
                  pl.BlockSpec((1, D_MODEL, D_FF), lambda i, e: (e, 0, 0)),
                  pl.BlockSpec((1, D_FF, D_MODEL), lambda i, e: (e, 0, 0))],
        out_specs=row,
        out_shape=jax.ShapeDtypeStruct((t, D_MODEL), F32),
        scratch_shapes=[pltpu.VMEM((tm, D_MODEL), F32), pltpu.VMEM((tm, D_MODEL), BF16),
                        pltpu.VMEM((tm, LANES), F32), pltpu.VMEM((tm, D_FF), BF16)],
        compiler_params=_params("arbitrary", "arbitrary"),
        name="moe",
    )(x, gain, router, wg, wu, wd)


def _pair_states(s):
    b = s.shape[0]
    s = s.reshape(b, RWKV_HEADS // 2, 2, RWKV_HEAD, RWKV_HEAD)
    z = jnp.zeros_like(s[:, :, 0])
    top = jnp.concatenate([s[:, :, 0], z], axis=-1)
    bot = jnp.concatenate([z, s[:, :, 1]], axis=-1)
    return jnp.concatenate([top, bot], axis=-2)


def _unpair_states(s):
    b = s.shape[0]
    s0 = s[:, :, :RWKV_HEAD, :RWKV_HEAD]
    s1 = s[:, :, RWKV_HEAD:, RWKV_HEAD:]
    return jnp.stack([s0, s1], axis=2).reshape(b, RWKV_HEADS, RWKV_HEAD, RWKV_HEAD)


def _trunk(x, ssm, conv, ck, cv, wkv, shift, w):
    b, l, _ = x.shape
    t = b * l
    ones_blk = w["ones_blk"]
    xf = x.reshape(t, D_MODEL)

    z, xbc, dt, q, k, v = _inproj(xf, w["norm_mix0"], w["wz"], w["wxbc"], w["wdt"], w["wq"], w["wk"], w["wv"],
                                  w["q_norm"], w["k_norm"], ones_blk)
    lp = l if l % SSD_CHUNK == 0 else 8
    seq = lambda a: a.reshape(b, l, a.shape[-1])
    padl = lambda a: a if lp == l else jnp.pad(a, ((0, 0), (0, lp - l), (0, 0)))
    y_ssd, ssm1, conv1 = _ssd(padl(seq(xbc)), padl(seq(dt)), padl(seq(z)), conv, ssm.reshape(b, SSD_WIDTH, SSD_STATE),
                              w["conv_w"], w["conv_b"], w["dt_bias"], w["a_log"], w["d_skip"], w["ssd_norm"], lv=l)
    y_ssd = y_ssd[:, :l].reshape(t, SSD_WIDTH)
    ssm1 = ssm1.reshape(b, SSD_HEADS, SSD_HEAD_DIM, SSD_STATE)
    if ck is None:
        o = _swa_prompt(seq(q), seq(k), seq(v), w["sinks"]).reshape(t, Q_DIM)
        k1 = seq(k)[:, -WINDOW:].reshape(b, WINDOW, ATT_KV_HEADS, ATT_HEAD_DIM)
        v1 = seq(v)[:, -WINDOW:].reshape(b, WINDOW, ATT_KV_HEADS, ATT_HEAD_DIM)
    else:
        wc = ck.shape[1]
        o, k1, v1 = _swa_cached(q, k, v, ck.reshape(b, wc, KV_DIM), cv.reshape(b, wc, KV_DIM), w["sinks"], l=l)
        k1 = k1.reshape(b, wc, ATT_KV_HEADS, ATT_HEAD_DIM)
        v1 = v1.reshape(b, wc, ATT_KV_HEADS, ATT_HEAD_DIM)
    x2 = _outproj_ffn(xf, y_ssd, o, w["wo_a"], w["wo_b"], w["norm_ffn0"], w["ffn_gate"], w["ffn_up"], w["ffn_down"])

    tm = min(TOKEN_TILE, t)
    if l >= tm:
        first = shift.reshape(b, 1, D_MODEL)
    else:
        first = jnp.repeat(shift, l, axis=0)
    r, lw, kx, vx, an, bb, gate, h1 = _rwkv_prep(
        x2, first, w["norm_mix1"], w["mu"], w["w0"], w["a0"], w["kk"], w["ka"], w["wr"], w["wkk"], w["wvv"],
        w["w1"], w["w2"], w["a1"], w["a2"], w["g1"], w["g2"], ones_blk, seq_len=l)
    shift1 = h1.reshape(b, l, D_MODEL)[:, -1]
    lpr = -(-l // RWKV_CHUNK) * RWKV_CHUNK
    padr = lambda a: seq(a) if lpr == l else jnp.pad(seq(a), ((0, 0), (0, lpr - l), (0, 0)))
    y, s1 = _rwkv_scan(padr(r), padr(lw), padr(kx), padr(vx), padr(an), padr(bb), _pair_states(wkv))
    y = y[:, :l].reshape(t, D_MODEL)
    x3 = _rwkv_out(x2, y, r, kx, vx, gate, w["ln_w"], w["ln_b"], w["rk"], w["wo"], ones_blk)
    x4 = _moe(x3, w["norm_ffn1"], w["router"], w["moe_gate"], w["moe_up"], w["moe_down"])
    return (x4.reshape(b, l, D_MODEL), ssm1[None], conv1[None], k1[None], v1[None],
            _unpair_states(s1)[None], shift1[None])


def kernel(x_prompt, x_sample, state_ssm, state_conv, cache_swa_k, cache_swa_v, state_wkv, state_shift, norm_mix, norm_ffn, w_in, conv_w, conv_b, dt_bias, a_log, d_skip, ssd_norm, q_norm, k_norm, attn_sinks, w_out, ffn_gate, ffn_up, ffn_down, rwkv_mu, rwkv_w0, rwkv_w1, rwkv_w2, rwkv_a0, rwkv_a1, rwkv_a2, rwkv_g1, rwkv_g2, rwkv_kk, rwkv_ka, rwkv_rk, rwkv_wr, rwkv_wk, rwkv_wv, rwkv_wo, rwkv_ln_w, rwkv_ln_b, moe_router, moe_gate, moe_up, moe_down):
    bf = lambda a: a.astype(BF16)
    row = lambda a: a.reshape(1, -1).astype(F32)
    padlane = lambda a: jnp.pad(a, ((0, 0), (0, LANES - a.shape[1])))
    wi = w_in[0]
    c0 = SSD_WIDTH
    c1 = c0 + CONV_DIM
    c2 = c1 + SSD_HEADS
    c3 = c2 + Q_DIM
    c4 = c3 + KV_DIM
    w = dict(
        ones_blk=_block_ones(),
        norm_mix0=row(norm_mix[0]), norm_mix1=row(norm_mix[1]), norm_ffn0=row(norm_ffn[0]), norm_ffn1=row(norm_ffn[1]),
        wz=bf(wi[:, :c0]), wxbc=bf(wi[:, c0:c1]), wdt=bf(padlane(wi[:, c1:c2])), wq=bf(wi[:, c2:c3]),
        wk=bf(wi[:, c3:c4]), wv=bf(wi[:, c4:]),
        q_norm=row(jnp.tile(q_norm[0], ATT_HEADS)), k_norm=row(jnp.tile(k_norm[0], ATT_KV_HEADS)),
        conv_w=conv_w[0], conv_b=row(conv_b[0]), dt_bias=padlane(row(dt_bias[0])), a_log=padlane(row(a_log[0])),
        d_skip=row(jnp.repeat(d_skip[0], SSD_HEAD_DIM)), ssd_norm=row(ssd_norm[0]), sinks=attn_sinks[0].astype(F32),
        wo_a=bf(w_out[0, :SSD_WIDTH]), wo_b=bf(w_out[0, SSD_WIDTH:]),
        ffn_gate=bf(ffn_gate[0]), ffn_up=bf(ffn_up[0]), ffn_down=bf(ffn_down[0]),
        mu=rwkv_mu[0], w0=row(rwkv_w0[0]), a0=row(rwkv_a0[0]), kk=row(rwkv_kk[0]), ka=row(rwkv_ka[0]),
        wr=bf(rwkv_wr[0]), wkk=bf(rwkv_wk[0]), wvv=bf(rwkv_wv[0]), wo=bf(rwkv_wo[0]),
        w1=bf(rwkv_w1[0]), w2=bf(rwkv_w2[0]), a1=bf(rwkv_a1[0]), a2=bf(rwkv_a2[0]),
        g1=bf(rwkv_g1[0]), g2=bf(rwkv_g2[0]),
        ln_w=row(rwkv_ln_w[0]), ln_b=row(rwkv_ln_b[0]), rk=row(rwkv_rk[0]),
        router=padlane(moe_router[0]), moe_gate=bf(moe_gate[0]), moe_up=bf(moe_up[0]), moe_down=bf(moe_down[0]),
    )
    bp = x_prompt.shape[0]
    z_ssm = jnp.zeros((bp,) + state_ssm.shape[2:], F32)
    z_conv = jnp.zeros((bp,) + state_conv.shape[2:], F32)
    z_wkv = jnp.zeros((bp,) + state_wkv.shape[2:], F32)
    z_shift = jnp.zeros((bp,) + state_shift.shape[2:], F32)
    outs_p = _trunk(x_prompt, z_ssm, z_conv, None, None, z_wkv, z_shift, w)
    outs_s = _trunk(x_sample, state_ssm[0], state_conv[0], cache_swa_k[0], cache_swa_v[0], state_wkv[0],
                    state_shift[0], w)
    return (outs_p[0], outs_s[0]) + tuple(outs_p[1:]) + tuple(outs_s[1:])
```

```python
import functools

import jax
import jax.numpy as jnp
from jax import lax
from jax.experimental import pallas as pl
from jax.experimental.pallas import tpu as pltpu

F32 = jnp.float32
BF16 = jnp.bfloat16

D_MODEL = 1024
SSD_HEAD_DIM = 64
SSD_HEADS = 16
SSD_GROUPS = 2
SSD_STATE = 128
SSD_CONV = 4
SSD_CHUNK = 128
SSD_WIDTH = 1024
CONV_DIM = SSD_WIDTH + 2 * SSD_GROUPS * SSD_STATE
ATT_HEAD_DIM = 64
ATT_HEADS = 16
ATT_KV_HEADS = 4
ATT_REP = ATT_HEADS // ATT_KV_HEADS
WINDOW = 128
Q_DIM = ATT_HEADS * ATT_HEAD_DIM
KV_DIM = ATT_KV_HEADS * ATT_HEAD_DIM
RWKV_HEAD = 64
RWKV_HEADS = 16
RWKV_GN_EPS = 64e-5
RWKV_CHUNK = 64
D_FF = 2816
N_EXPERTS = 8
RMS_EPS = 1e-6
NEG_INF = -1e30

LANES = 128
MXU_DIM = 256
VMEM_LIMIT_BYTES = 56 * 1024 * 1024
TOKEN_TILE = 512
FF_CHUNK = MXU_DIM
HIGHEST = lax.Precision.HIGHEST


def _dot(a, b, precision=None):
    return jnp.dot(a, b, preferred_element_type=F32, precision=precision)


def _dot_nt(a, b, precision=None):
    return lax.dot_general(a, b, (((1,), (1,)), ((), ())), preferred_element_type=F32, precision=precision)


def _dot_tn(a, b, precision=None):
    return lax.dot_general(a, b, (((0,), (0,)), ((), ())), preferred_element_type=F32, precision=precision)


def _rms_rows(x, g):
    return x * lax.rsqrt(jnp.mean(x * x, axis=-1, keepdims=True) + RMS_EPS) * g


def _sigmoid(x):
    return 1.0 / (1.0 + jnp.exp(-x))


def _silu(x):
    return x * _sigmoid(x)


def _softplus(x):
    return jnp.maximum(x, 0.0) + jnp.log(1.0 + jnp.exp(-jnp.abs(x)))


def _const_spec(shape):
    zeros = (0,) * len(shape)
    return pl.BlockSpec(shape, lambda *_: zeros, pipeline_mode=pl.Buffered(1))


def _params(*semantics):
    return pltpu.CompilerParams(dimension_semantics=semantics, vmem_limit_bytes=VMEM_LIMIT_BYTES)


def _head_sums(t, ones_blk):
    return _dot(t.astype(BF16), ones_blk)


def _block_ones():
    r = jnp.arange(MXU_DIM) // ATT_HEAD_DIM
    return (r[:, None] == r[None, :]).astype(BF16)


def _inproj_kernel(x_ref, g_ref, wz_ref, wxbc_ref, wdt_ref, wq_ref, wk_ref, wv_ref, qn_ref, kn_ref, ones_ref,
                   z_ref, xbc_ref, dt_ref, q_ref, k_ref, v_ref):
    h = _rms_rows(x_ref[...], g_ref[...]).astype(BF16)
    ones_blk = ones_ref[...]

    def head_rms(t, gain):
        outs = []
        for c in range(t.shape[1] // MXU_DIM):
            tc = t[:, c * MXU_DIM:(c + 1) * MXU_DIM]
            ms = _head_sums(tc * tc, ones_blk) * (1.0 / ATT_HEAD_DIM)
            outs.append(tc * lax.rsqrt(ms + RMS_EPS))
        return jnp.concatenate(outs, axis=1) * gain

    z_ref[...] = _dot(h, wz_ref[...]).astype(z_ref.dtype)
    xbc_ref[...] = _dot(h, wxbc_ref[...])
    dt_ref[...] = _dot(h, wdt_ref[...])
    q_ref[...] = head_rms(_dot(h, wq_ref[...]), qn_ref[...]).astype(q_ref.dtype)
    k_ref[...] = head_rms(_dot(h, wk_ref[...]), kn_ref[...])
    v_ref[...] = _dot(h, wv_ref[...])


def _inproj(x, gain, wz, wxbc, wdt, wq, wk, wv, qn, kn, ones_blk):
    t = x.shape[0]
    tm = min(TOKEN_TILE, t)
    row = lambda n: pl.BlockSpec((tm, n), lambda i: (i, 0))
    return pl.pallas_call(
        _inproj_kernel,
        grid=(t // tm,),
        in_specs=[row(D_MODEL), _const_spec((1, D_MODEL)),
                  _const_spec(wz.shape), _const_spec(wxbc.shape), _const_spec(wdt.shape),
                  _const_spec(wq.shape), _const_spec(wk.shape), _const_spec(wv.shape),
                  _const_spec((1, Q_DIM)), _const_spec((1, KV_DIM)), _const_spec((MXU_DIM, MXU_DIM))],
        out_specs=[row(SSD_WIDTH), row(CONV_DIM), row(LANES), row(Q_DIM), row(KV_DIM), row(KV_DIM)],
        out_shape=[jax.ShapeDtypeStruct((t, SSD_WIDTH), BF16), jax.ShapeDtypeStruct((t, CONV_DIM), F32),
                   jax.ShapeDtypeStruct((t, LANES), F32), jax.ShapeDtypeStruct((t, Q_DIM), BF16),
                   jax.ShapeDtypeStruct((t, KV_DIM), F32), jax.ShapeDtypeStruct((t, KV_DIM), F32)],
        compiler_params=_params("arbitrary"),
        name="inproj",
    )(x, gain, wz, wxbc, wdt, wq, wk, wv, qn, kn, ones_blk)


def _ssd_kernel(xbc_ref, dt_ref, z_ref, conv0_ref, ssm0_ref, cw_ref, cb_ref, dtb_ref, alog_ref, dsk_ref, nrm_ref,
                y_ref, ssm1_ref, conv1_ref, xpad_ref, dtpad_ref, s_ref, *, q, lin, lv, nchunks):
    c = pl.program_id(1)
    n_pairs = SSD_HEADS // 2
    hist = SSD_CONV - 1
    base = 8 - hist

    @pl.when(c == 0)
    def _():
        xpad_ref[...] = jnp.zeros(xpad_ref.shape, F32)
        dtpad_ref[...] = jnp.zeros(dtpad_ref.shape, F32)
        xpad_ref[base:8, :] = conv0_ref[0]
        for jb in range(n_pairs):
            s_ref[:, jb * LANES:(jb + 1) * LANES] = ssm0_ref[0, jb * LANES:(jb + 1) * LANES, :].T

    xpad_ref[8:8 + lin, :] = xbc_ref[0]
    dtpad_ref[0:lin, :] = dt_ref[0]

    conv = cb_ref[...]
    for j in range(SSD_CONV):
        conv = conv + xpad_ref[base + j:base + j + q, :] * cw_ref[j:j + 1, :]
    tail = xpad_ref[base + lv:8 + lv, :]
    conv1_ref[0] = tail
    xpad_ref[base:8, :] = tail

    xc = _silu(conv)
    xs = xc[:, :SSD_WIDTH]
    bm = xc[:, SSD_WIDTH:SSD_WIDTH + SSD_GROUPS * SSD_STATE].astype(BF16)
    cm = xc[:, SSD_WIDTH + SSD_GROUPS * SSD_STATE:].astype(BF16)

    row_i = lax.broadcasted_iota(jnp.int32, (q, LANES), 0)
    col_i = lax.broadcasted_iota(jnp.int32, (q, LANES), 1)
    first_half = col_i < SSD_HEAD_DIM
    dt = jnp.where(row_i < lv, _softplus(dtpad_ref[...] + dtb_ref[...]), 0.0)
    a = dt * (-jnp.exp(alog_ref[...]))
    r_q = lax.broadcasted_iota(jnp.int32, (q, q), 0)
    c_q = lax.broadcasted_iota(jnp.int32, (q, q), 1)
    causal = r_q >= c_q
    a_cs = _dot(causal.astype(F32), a, precision=HIGHEST)
    a_cs_t = a_cs.T
    ea = jnp.exp(a_cs)
    dte = jnp.exp(a_cs[q - 1:q, :] - a_cs)

    def colb(arr, h):
        return jnp.broadcast_to(arr[:, h:h + 1], (q, LANES))

    def pairb(arr, h0):
        return jnp.where(first_half, colb(arr, h0), colb(arr, h0 + 1))

    cbs = []
    for g in range(SSD_GROUPS):
        sl = slice(g * SSD_STATE, (g + 1) * SSD_STATE)
        cbs.append(_dot_nt(cm[:, sl], bm[:, sl]))

    ys = []
    for j in range(n_pairs):
        g = (2 * j) // (SSD_HEADS // SSD_GROUPS)
        gsl = slice(g * SSD_STATE, (g + 1) * SSD_STATE)
        psl = slice(j * LANES, (j + 1) * LANES)
        xs_p = xs[:, psl]
        xdt = xs_p * pairb(dt, 2 * j)
        xdt_b = xdt.astype(BF16)
        yd = []
        for hh in (2 * j, 2 * j + 1):
            diff = colb(a_cs, hh) - a_cs_t[hh:hh + 1, :]
            lmat = jnp.exp(jnp.where(causal, diff, NEG_INF))
            yd.append(_dot((cbs[g] * lmat).astype(BF16), xdt_b))
        y_diag = jnp.where(first_half, yd[0], yd[1])
        ea_p = pairb(ea, 2 * j)
        s_old = s_ref[:, psl]
        y_off = _dot(cm[:, gsl], s_old.astype(BF16)) * ea_p
        s_ref[:, psl] = s_old * ea_p[q - 1:q, :] + _dot_tn(bm[:, gsl], (xdt * pairb(dte, 2 * j)).astype(BF16))
        ys.append(y_diag + y_off + xs_p * dsk_ref[:, psl])
    y = jnp.concatenate(ys, axis=1)[0:lin]

    y = y * _silu(z_ref[0].astype(F32))
    gw = SSD_WIDTH // SSD_GROUPS
    outs = []
    for g in range(SSD_GROUPS):
        yg = y[:, g * gw:(g + 1) * gw]
        outs.append(yg * lax.rsqrt(jnp.mean(yg * yg, axis=-1, keepdims=True) + RMS_EPS))
    y_ref[0] = (jnp.concatenate(outs, axis=1) * nrm_ref[...]).astype(y_ref.dtype)

    @pl.when(c == nchunks - 1)
    def _():
        for jb in range(n_pairs):
            ssm1_ref[0, jb * LANES:(jb + 1) * LANES, :] = s_ref[:, jb * LANES:(jb + 1) * LANES].T


def _ssd(xbc, dt, z, conv0, ssm0, cw, cb, dtb, alog, dsk, nrm, *, lv):
    b, lp, _ = xbc.shape
    q = SSD_CHUNK
    lin = min(q, lp)
    nchunks = lp // lin
    lv_chunk = min(lv, lin)
    kern = functools.partial(_ssd_kernel, q=q, lin=lin, lv=lv_chunk, nchunks=nchunks)
    seq = lambda n: pl.BlockSpec((1, lin, n), lambda i, c: (i, c, 0))
    per_b = lambda s: pl.BlockSpec((1,) + s, lambda i, c: (i, 0, 0))
    return pl.pallas_call(
        kern,
        grid=(b, nchunks),
        in_specs=[seq(CONV_DIM), seq(LANES), seq(SSD_WIDTH),
                  per_b((SSD_CONV - 1, CONV_DIM)), per_b((SSD_WIDTH, SSD_STATE)),
                  _const_spec((SSD_CONV, CONV_DIM)), _const_spec((1, CONV_DIM)), _const_spec((1, LANES)),
                  _const_spec((1, LANES)), _const_spec((1, SSD_WIDTH)), _const_spec((1, SSD_WIDTH))],
        out_specs=[seq(SSD_WIDTH), per_b((SSD_WIDTH, SSD_STATE)), per_b((SSD_CONV - 1, CONV_DIM))],
        out_shape=[jax.ShapeDtypeStruct((b, lp, SSD_WIDTH), BF16),
                   jax.ShapeDtypeStruct((b, SSD_WIDTH, SSD_STATE), F32),
                   jax.ShapeDtypeStruct((b, SSD_CONV - 1, CONV_DIM), F32)],
        scratch_shapes=[pltpu.VMEM((q + 8, CONV_DIM), F32), pltpu.VMEM((q, LANES), F32),
                        pltpu.VMEM((SSD_STATE, SSD_WIDTH), F32)],
        compiler_params=_params("arbitrary", "arbitrary"),
        name="ssd",
    )(xbc, dt, z, conv0, ssm0, cw, cb, dtb, alog, dsk, nrm)


def _sink_softmax(s, allowed, sink_col):
    s = jnp.where(allowed, s, NEG_INF)
    m = jnp.maximum(jnp.max(s, axis=-1, keepdims=True), sink_col)
    p = jnp.exp(s - m)
    denom = jnp.sum(p, axis=-1, keepdims=True) + jnp.exp(sink_col - m)
    return p.astype(BF16), 1.0 / denom


def _gqa_blocks(blocks, sink_ref, allowed, nq):
    d = ATT_HEAD_DIM
    heads = range(ATT_KV_HEADS)
    sinks = [jnp.concatenate([jnp.full((nq, 1), sink_ref[ATT_REP * j + r], F32) for r in range(ATT_REP)], axis=0)
             for j in heads]
    scores = []
    for qb, kk, _ in blocks:
        for j in heads:
            qj = jnp.concatenate(
                [qb[:, (ATT_REP * j + r) * d:(ATT_REP * j + r + 1) * d] for r in range(ATT_REP)], axis=0)
            scores.append(_dot_nt(qj, kk[:, j * d:(j + 1) * d]) * (d ** -0.5))
    probs = [_sink_softmax(s, allowed, sinks[n % ATT_KV_HEADS]) for n, s in enumerate(scores)]
    results = []
    for bi, (_, _, vv) in enumerate(blocks):
        outs = []
        for j in heads:
            p, inv = probs[bi * ATT_KV_HEADS + j]
            o = _dot(p, vv[:, j * d:(j + 1) * d]) * inv
            outs.extend(o[r * nq:(r + 1) * nq] for r in range(ATT_REP))
        results.append(jnp.concatenate(outs, axis=1))
    return results


def _swa_prompt_kernel(sink_ref, q_ref, kp_ref, kc_ref, vp_ref, vc_ref, o_ref):
    i = pl.program_id(1)
    w = WINDOW
    kk = jnp.concatenate([kp_ref[0], kc_ref[0]], axis=0).astype(BF16)
    vv = jnp.concatenate([vp_ref[0], vc_ref[0]], axis=0).astype(BF16)
    rows = lax.broadcasted_iota(jnp.int32, (ATT_REP * w, 2 * w), 0) % w
    cols = lax.broadcasted_iota(jnp.int32, (ATT_REP * w, 2 * w), 1)
    rel = rows + w - cols
    allowed = (rel >= 0) & (rel <= w) & ((cols >= w) | (i > 0))
    o_ref[0] = _gqa_blocks([(q_ref[0], kk, vv)], sink_ref, allowed, w)[0].astype(o_ref.dtype)


def _swa_prompt(q, k, v, sinks):
    b, l, _ = q.shape
    nb = l // WINDOW
    cur = lambda n: pl.BlockSpec((1, WINDOW, n), lambda i, c: (i, c, 0))
    prev = lambda n: pl.BlockSpec((1, WINDOW, n), lambda i, c: (i, jnp.maximum(c - 1, 0), 0))
    return pl.pallas_call(
        _swa_prompt_kernel,
        grid=(b, nb),
        in_specs=[pl.BlockSpec(memory_space=pltpu.SMEM),
                  cur(Q_DIM), prev(KV_DIM), cur(KV_DIM), prev(KV_DIM), cur(KV_DIM)],
        out_specs=cur(Q_DIM),
        out_shape=jax.ShapeDtypeStruct((b, l, Q_DIM), BF16),
        compiler_params=_params("arbitrary", "arbitrary"),
        name="swa_prompt",
    )(sinks, q, k, k, v, v)


def _swa_cached_kernel(sink_ref, q_ref, k_ref, v_ref, kbuf_ref, vbuf_ref, o_ref, k1_ref, v1_ref, *, nbatch, l):
    wc = kbuf_ref.shape[1]
    d = ATT_HEAD_DIM
    pad = 8 - l
    nk = wc + 8
    rows = lax.broadcasted_iota(jnp.int32, (ATT_REP * l, nk), 0) % l
    cols = lax.broadcasted_iota(jnp.int32, (ATT_REP * l, nk), 1)
    rel = rows + wc - cols
    allowed = (rel >= 0) & (rel <= WINDOW)
    zpad = jnp.zeros((pad, KV_DIM), F32)
    qall = q_ref[...].astype(F32)
    blocks = []
    for bi in range(nbatch):
        kn = k_ref[bi * l:(bi + 1) * l, :]
        vn = v_ref[bi * l:(bi + 1) * l, :]
        kc = jnp.concatenate([kbuf_ref[bi], kn, zpad], axis=0).astype(BF16)
        vc = jnp.concatenate([vbuf_ref[bi], vn, zpad], axis=0).astype(BF16)
        blocks.append((qall[bi * l:(bi + 1) * l, :].astype(BF16), kc, vc))
        k1_ref[bi, 0:wc - l, :] = kbuf_ref[bi, l:wc, :]
        k1_ref[bi, wc - l:wc, :] = kn
        v1_ref[bi, 0:wc - l, :] = vbuf_ref[bi, l:wc, :]
        v1_ref[bi, wc - l:wc, :] = vn
    o_ref[...] = jnp.concatenate(_gqa_blocks(blocks, sink_ref, allowed, l), axis=0).astype(o_ref.dtype)


def _swa_cached(q, k, v, kbuf, vbuf, sinks, *, l):
    b, wc, _ = kbuf.shape
    nbatch = 8
    kern = functools.partial(_swa_cached_kernel, nbatch=nbatch, l=l)
    rows = lambda n: pl.BlockSpec((nbatch * l, n), lambda i: (i, 0))
    cache = pl.BlockSpec((nbatch, wc, KV_DIM), lambda i: (i, 0, 0))
    return pl.pallas_call(
        kern,
        grid=(b // nbatch,),
        in_specs=[pl.BlockSpec(memory_space=pltpu.SMEM), rows(Q_DIM), rows(KV_DIM), rows(KV_DIM), cache, cache],
        out_specs=[rows(Q_DIM), cache, cache],
        out_shape=[jax.ShapeDtypeStruct((b * l, Q_DIM), BF16),
                   jax.ShapeDtypeStruct((b, wc, KV_DIM), F32), jax.ShapeDtypeStruct((b, wc, KV_DIM), F32)],
        compiler_params=_params("arbitrary"),
        name="swa_cached",
    )(sinks, q, k, v, kbuf, vbuf)


def _swiglu_act(h, wg_ref, wu_ref, act_ref):
    for c in range(D_FF // FF_CHUNK):
        sl = slice(c * FF_CHUNK, (c + 1) * FF_CHUNK)
        act_ref[:, sl] = (_silu(_dot(h, wg_ref[:, sl])) * _dot(h, wu_ref[:, sl])).astype(BF16)


def _outproj_ffn_kernel(x_ref, ya_ref, yb_ref, woa_ref, wob_ref, g_ref, wg_ref, wu_ref, wd_ref, o_ref, act_ref):
    x1 = x_ref[...] + _dot(ya_ref[...], woa_ref[...]) + _dot(yb_ref[...], wob_ref[...])
    h = _rms_rows(x1, g_ref[...]).astype(BF16)
    _swiglu_act(h, wg_ref, wu_ref, act_ref)
    o_ref[...] = x1 + _dot(act_ref[...], wd_ref[...])


def _outproj_ffn(x, ya, yb, woa, wob, gain, wg, wu, wd):
    t = x.shape[0]
    tm = min(TOKEN_TILE, t)
    row = lambda n: pl.BlockSpec((tm, n), lambda i: (i, 0))
    return pl.pallas_call(
        _outproj_ffn_kernel,
        grid=(t // tm,),
        in_specs=[row(D_MODEL), row(SSD_WIDTH), row(Q_DIM), _const_spec(woa.shape), _const_spec(wob.shape),
                  _const_spec((1, D_MODEL)), _const_spec(wg.shape), _const_spec(wu.shape), _const_spec(wd.shape)],
        out_specs=row(D_MODEL),
        out_shape=jax.ShapeDtypeStruct((t, D_MODEL), F32),
        scratch_shapes=[pltpu.VMEM((tm, D_FF), BF16)],
        compiler_params=_params("arbitrary"),
        name="outproj_ffn",
    )(x, ya, yb, woa, wob, gain, wg, wu, wd)


def _rwkv_prep_kernel(x_ref, first_ref, g_ref, mu_ref, w0_ref, a0_ref, kkc_ref, ka_ref,
                      wr_ref, wk_ref, wv_ref, w1_ref, w2_ref, a1_ref, a2_ref, g1_ref, g2_ref, ones_ref,
                      r_ref, lw_ref, k_ref, v_ref, an_ref, b_ref, gate_ref, h_ref, hs_ref, *, tm, seq_len):
    i = pl.program_id(0)
    h = _rms_rows(x_ref[...], g_ref[...])
    h_ref[...] = h
    tiles_per_seq = max(seq_len // tm, 1)

    if seq_len >= tm:
        @pl.when(i % tiles_per_seq == 0)
        def _():
            hs_ref[7:8, :] = first_ref[0]
    hs_ref[8:8 + tm, :] = h
    prev = hs_ref[7:7 + tm, :]
    if seq_len < tm:
        rows = lax.broadcasted_iota(jnp.int32, (tm, D_MODEL), 0)
        prev = jnp.where(rows % seq_len == 0, first_ref[...], prev)
    else:
        hs_ref[7:8, :] = h[tm - 1:tm, :]
    xx = prev - h
    mix = lambda n: (h + xx * mu_ref[n:n + 1, :]).astype(BF16)
    r = _dot(mix(0), wr_ref[...])
    k = _dot(mix(2), wk_ref[...])
    v = _dot(mix(3), wv_ref[...])
    w_lora = _dot(jnp.tanh(_dot(mix(1), w1_ref[...])).astype(BF16), w2_ref[...])
    a_lora = _dot(_dot(mix(4), a1_ref[...]).astype(BF16), a2_ref[...])
    gate = _dot(_sigmoid(_dot(mix(5), g1_ref[...])).astype(BF16), g2_ref[...])
    w_log = -_softplus(-(w0_ref[...] + w_lora)) - 0.5
    a_sig = _sigmoid(a0_ref[...] + a_lora)
    kk = k * kkc_ref[...]
    ones_blk = ones_ref[...]
    parts = []
    for c in range(D_MODEL // MXU_DIM):
        kc = kk[:, c * MXU_DIM:(c + 1) * MXU_DIM]
        nrm = jnp.maximum(jnp.sqrt(_head_sums(kc * kc, ones_blk)), 1e-12)
        parts.append(kc / nrm)
    kk = jnp.concatenate(parts, axis=1)
    r_ref[...] = r
    lw_ref[...] = -jnp.exp(w_log)
    k_ref[...] = k * (1.0 + (a_sig - 1.0) * ka_ref[...])
    v_ref[...] = v
    an_ref[...] = -kk
    b_ref[...] = kk * a_sig
    gate_ref[...] = gate.astype(gate_ref.dtype)


def _rwkv_prep(x, first, gain, mu, w0, a0, kkc, ka, wr, wk, wv, w1, w2, a1, a2, g1, g2, ones_blk, *, seq_len):
    t = x.shape[0]
    tm = min(TOKEN_TILE, t)
    kern = functools.partial(_rwkv_prep_kernel, tm=tm, seq_len=seq_len)
    row = lambda n: pl.BlockSpec((tm, n), lambda i: (i, 0))
    if seq_len >= tm:
        tiles_per_seq = seq_len // tm
        first_spec = pl.BlockSpec((1, 1, D_MODEL), lambda i: (i // tiles_per_seq, 0, 0))
    else:
        first_spec = row(D_MODEL)
    vec = _const_spec((1, D_MODEL))
    outs = [jax.ShapeDtypeStruct((t, D_MODEL), F32)] * 6 + [
        jax.ShapeDtypeStruct((t, D_MODEL), BF16), jax.ShapeDtypeStruct((t, D_MODEL), F32)]
    return pl.pallas_call(
        kern,
        grid=(t // tm,),
        in_specs=[row(D_MODEL), first_spec, vec, _const_spec((6, D_MODEL)), vec, vec, vec, vec,
                  _const_spec(wr.shape), _const_spec(wk.shape), _const_spec(wv.shape),
                  _const_spec(w1.shape), _const_spec(w2.shape), _const_spec(a1.shape), _const_spec(a2.shape),
                  _const_spec(g1.shape), _const_spec(g2.shape), _const_spec((MXU_DIM, MXU_DIM))],
        out_specs=[row(D_MODEL)] * 8,
        out_shape=outs,
        scratch_shapes=[pltpu.VMEM((tm + 8, D_MODEL), F32)],
        compiler_params=_params("arbitrary"),
        name="rwkv_prep",
    )(x, first, gain, mu, w0, a0, kkc, ka, wr, wk, wv, w1, w2, a1, a2, g1, g2, ones_blk)


def _rwkv_scan_kernel(r_ref, lw_ref, k_ref, v_ref, a_ref, b_ref, s0_ref, y_ref, s1_ref, s_ref, *, nsub, nsteps):
    step = pl.program_id(1)
    c = RWKV_CHUNK
    n_pairs = RWKV_HEADS // 2
    two_c = 2 * c

    @pl.when(step == 0)
    def _():
        s_ref[...] = s0_ref[0]

    lane = lax.broadcasted_iota(jnp.int32, (c, LANES), 1)
    m_a = lane < RWKV_HEAD
    r_i = lax.broadcasted_iota(jnp.int32, (two_c, two_c), 0)
    c_i = lax.broadcasted_iota(jnp.int32, (two_c, two_c), 1)
    strict = r_i > c_i
    blk = (r_i < c) == (c_i < c)
    r_i2 = lax.broadcasted_iota(jnp.int32, (two_c, 2 * two_c), 0)
    c_i2 = lax.broadcasted_iota(jnp.int32, (two_c, 2 * two_c), 1) % two_c
    incl2 = r_i2 >= c_i2
    t_r = lax.broadcasted_iota(jnp.int32, (c, c), 0)
    t_c = lax.broadcasted_iota(jnp.int32, (c, c), 1)
    tri = (t_r >= t_c).astype(F32)

    def stack(x):
        return jnp.concatenate([jnp.where(m_a, x, 0.0), jnp.where(m_a, 0.0, x)], axis=0)

    def sub_chunk(ci, carry):
        rows = pl.ds(pl.multiple_of(ci * c, c), c)
        lw = lw_ref[0, rows, :]
        cl = _dot(tri, lw, precision=HIGHEST)
        cl_end = cl[c - 1:c, :]
        e_neg = jnp.exp(-cl)
        e_end = jnp.exp(cl_end - cl)
        w_end = jnp.exp(cl_end)
        aa = a_ref[0, rows, :] * jnp.exp(cl - lw)
        rr = r_ref[0, rows, :] * jnp.exp(cl)
        bb = b_ref[0, rows, :]
        kk = k_ref[0, rows, :]
        vv = v_ref[0, rows, :]
        pairs = range(n_pairs)
        sl = [slice(p * LANES, (p + 1) * LANES) for p in pairs]
        lhs = [jnp.concatenate([stack(aa[:, sl[p]]), stack(rr[:, sl[p]])], axis=0).astype(BF16) for p in pairs]
        rhs = [jnp.concatenate([stack((bb * e_neg)[:, sl[p]]), stack((kk * e_neg)[:, sl[p]])], axis=0).astype(BF16)
               for p in pairs]
        bk = [jnp.concatenate([stack((bb * e_end)[:, sl[p]]), stack((kk * e_end)[:, sl[p]])], axis=0).astype(BF16)
              for p in pairs]
        v_s = [stack(vv[:, sl[p]]) for p in pairs]
        s_old = [s_ref[p] for p in pairs]
        gmat = [_dot_nt(lhs[p], rhs[p]) for p in pairs]
        ph = [_dot_nt(lhs[p], s_old[p].astype(BF16)) for p in pairs]
        npow = [jnp.where(strict, gmat[p][:two_c, :two_c], 0.0).astype(BF16) for p in pairs]
        a_ak = [jnp.where(strict, gmat[p][:two_c, two_c:], 0.0).astype(BF16) for p in pairs]
        a_r = [jnp.where(incl2, gmat[p][two_c:, :], 0.0).astype(BF16) for p in pairs]
        u = [ph[p][:two_c] + _dot(a_ak[p], v_s[p].astype(BF16)) for p in pairs]
        for lvl in range(6):
            u = [u[p] + _dot(npow[p], u[p].astype(BF16)) for p in pairs]
            if lvl < 5:
                npow = [_dot(npow[p], npow[p]).astype(BF16) for p in pairs]
        uv = [jnp.concatenate([u[p], v_s[p]], axis=0).astype(BF16) for p in pairs]
        y_s = [ph[p][two_c:] + _dot(a_r[p], uv[p]) for p in pairs]
        y_ref[0, rows, :] = jnp.concatenate([y_s[p][:c] + y_s[p][c:] for p in pairs], axis=1)
        for p in pairs:
            s_new = s_old[p] * w_end[:, sl[p]] + _dot_tn(uv[p], bk[p])
            s_ref[p] = jnp.where(blk, s_new, 0.0)
        return carry

    lax.fori_loop(0, nsub, sub_chunk, 0)

    @pl.when(step == nsteps - 1)
    def _():
        s1_ref[0] = s_ref[...]


def _rwkv_scan(r, lw, k, v, an, b, s0):
    bsz, lp, _ = r.shape
    rows = min(lp, 512)
    nsteps = lp // rows
    nsub = rows // RWKV_CHUNK
    kern = functools.partial(_rwkv_scan_kernel, nsub=nsub, nsteps=nsteps)
    seq = pl.BlockSpec((1, rows, D_MODEL), lambda i, s: (i, s, 0))
    st = pl.BlockSpec((1, RWKV_HEADS // 2, LANES, LANES), lambda i, s: (i, 0, 0, 0))
    return pl.pallas_call(
        kern,
        grid=(bsz, nsteps),
        in_specs=[seq] * 6 + [st],
        out_specs=[seq, st],
        out_shape=[jax.ShapeDtypeStruct((bsz, lp, D_MODEL), F32),
                   jax.ShapeDtypeStruct((bsz, RWKV_HEADS // 2, LANES, LANES), F32)],
        scratch_shapes=[pltpu.VMEM((RWKV_HEADS // 2, LANES, LANES), F32)],
        compiler_params=_params("arbitrary", "arbitrary"),
        name="rwkv_scan",
    )(r, lw, k, v, an, b, s0)


def _rwkv_out_kernel(x_ref, y_ref, r_ref, k_ref, v_ref, gate_ref, lnw_ref, lnb_ref, rk_ref, wo_ref, ones_ref, o_ref):
    ones_blk = ones_ref[...]
    inv = 1.0 / RWKV_HEAD
    parts = []
    for c in range(D_MODEL // MXU_DIM):
        sl = slice(c * MXU_DIM, (c + 1) * MXU_DIM)
        y = y_ref[:, sl]
        yh = y.astype(BF16)
        mean = (_dot(yh, ones_blk) + _head_sums(y - yh.astype(F32), ones_blk)) * inv
        yc = y - mean
        var = _head_sums(yc * yc, ones_blk) * inv
        yn = yc * lax.rsqrt(var + RWKV_GN_EPS) * lnw_ref[:, sl] + lnb_ref[:, sl]
        rk = r_ref[:, sl] * k_ref[:, sl] * rk_ref[:, sl]
        rkh = rk.astype(BF16)
        bonus = _dot(rkh, ones_blk) + _head_sums(rk - rkh.astype(F32), ones_blk)
        parts.append(yn + bonus * v_ref[:, sl])
    out = jnp.concatenate(parts, axis=1) * gate_ref[...].astype(F32)
    o_ref[...] = x_ref[...] + _dot(out.astype(BF16), wo_ref[...])


def _rwkv_out(x, y, r, k, v, gate, lnw, lnb, rk, wo, ones_blk):
    t = x.shape[0]
    tm = min(TOKEN_TILE, t)
    row = pl.BlockSpec((tm, D_MODEL), lambda i: (i, 0))
    vec = _const_spec((1, D_MODEL))
    return pl.pallas_call(
        _rwkv_out_kernel,
        grid=(t // tm,),
        in_specs=[row] * 6 + [vec, vec, vec, _const_spec(wo.shape), _const_spec((MXU_DIM, MXU_DIM))],
        out_specs=row,
        out_shape=jax.ShapeDtypeStruct((t, D_MODEL), F32),
        compiler_params=_params("arbitrary"),
        name="rwkv_out",
    )(x, y, r, k, v, gate, lnw, lnb, rk, wo, ones_blk)


def _moe_route_kernel(x_ref, g_ref, router_ref, h_ref, comb_ref, mask_ref):
    h = _rms_rows(x_ref[...], g_ref[...])
    h_ref[...] = h
    logits = _dot(h, router_ref[...], precision=HIGHEST)
    lane = lax.broadcasted_iota(jnp.int32, logits.shape, 1)
    logits = jnp.where(lane < N_EXPERTS, logits, -jnp.inf)
    m1 = jnp.max(logits, axis=-1, keepdims=True)
    i1 = jnp.min(jnp.where(logits == m1, lane, LANES), axis=-1, keepdims=True)
    rest = jnp.where(lane == i1, -jnp.inf, logits)
    m2 = jnp.max(rest, axis=-1, keepdims=True)
    i2 = jnp.min(jnp.where(rest == m2, lane, LANES), axis=-1, keepdims=True)
    e2 = jnp.exp(m2 - m1)
    g1 = 1.0 / (1.0 + e2)
    g2 = e2 / (1.0 + e2)
    comb_ref[...] = jnp.where(lane == i1, g1, 0.0) + jnp.where(lane == i2, g2, 0.0)
    mask_ref[...] = jnp.where((lane == i1) | (lane == i2), 1.0, 0.0)


def _moe_route(x, gain, router):
    t = x.shape[0]
    tm = min(TOKEN_TILE, t)
    row = lambda n: pl.BlockSpec((tm, n), lambda i: (i, 0))
    return pl.pallas_call(
        _moe_route_kernel,
        grid=(t // tm,),
        in_specs=[row(D_MODEL), _const_spec((1, D_MODEL)), _const_spec((D_MODEL, LANES))],
        out_specs=[row(D_MODEL), row(LANES), row(LANES)],
        out_shape=[jax.ShapeDtypeStruct((t, D_MODEL), F32), jax.ShapeDtypeStruct((t, LANES), F32),
                   jax.ShapeDtypeStruct((t, LANES), F32)],
        compiler_params=_params("arbitrary"),
        name="moe_route",
    )(x, gain, router)


def _moe_plan(mask, comb, tm):
    m = mask[:, :N_EXPERTS].astype(jnp.int32)
    t = m.shape[0]
    n_tiles = (2 * t) // tm + N_EXPERTS
    rank = jnp.cumsum(m, axis=0)
    counts = rank[-1]
    tiles_e = (counts + tm - 1) // tm
    tile_end = jnp.cumsum(tiles_e)
    tile_start = tile_end - tiles_e
    pos = (tile_start * tm)[None, :] + rank - 1
    slot = jnp.cumsum(m, axis=1) - 1
    dst = 2 * jnp.arange(t, dtype=jnp.int32)[:, None] + slot
    table = jnp.zeros((n_tiles * tm,), jnp.int32).at[jnp.where(m > 0, pos, n_tiles * tm)].set(dst, mode="drop")
    tile_ids = jnp.arange(n_tiles, dtype=jnp.int32)
    texp = jnp.sum((tile_ids[:, None] >= tile_end[None, :]).astype(jnp.int32), axis=1)
    texp_c = jnp.minimum(texp, N_EXPERTS - 1)
    nvalid = jnp.clip(counts[texp_c] - (tile_ids - tile_start[texp_c]) * tm, 0, tm)
    nvalid = jnp.where(texp < N_EXPERTS, nvalid, 0).astype(jnp.int32)
    c8 = comb[:, :N_EXPERTS]
    gates = jnp.stack([jnp.sum(jnp.where((slot == k) & (m > 0), c8, 0.0), axis=1) for k in range(2)], axis=1)
    return texp_c.astype(jnp.int32), nvalid, table.reshape(n_tiles, 1, tm), gates


def _moe_ffn_kernel(texp_ref, nvalid_ref, tab_ref, h_hbm, wg_ref, wu_ref, wd_ref, y_hbm,
                    xbuf, ybuf, act_ref, gsem, ssem):
    i = pl.program_id(0)
    n = nvalid_ref[i]

    @pl.when(i == 0)
    def _():
        xbuf[...] = jnp.zeros(xbuf.shape, F32)

    def gather(j):
        tok = lax.shift_right_logical(tab_ref[0, 0, j], 1)
        return pltpu.make_async_copy(h_hbm.at[pl.ds(tok, 1), :], xbuf.at[pl.ds(j, 1), :], gsem)

    def scatter(j):
        return pltpu.make_async_copy(ybuf.at[pl.ds(j, 1), :], y_hbm.at[pl.ds(tab_ref[0, 0, j], 1), :], ssem)

    def for_rows(fn):
        def body(j, c):
            fn(j)
            return c
        lax.fori_loop(0, n, body, 0)

    @pl.when(n > 0)
    def _():
        for_rows(lambda j: gather(j).start())
        for_rows(lambda j: gather(j).wait())
        _swiglu_act(xbuf[...].astype(BF16), wg_ref.at[0], wu_ref.at[0], act_ref)
        ybuf[...] = _dot(act_ref[...], wd_ref[0])
        for_rows(lambda j: scatter(j).start())
        for_rows(lambda j: scatter(j).wait())


def _moe_ffn(texp, nvalid, table, h, wg, wu, wd):
    n_tiles, _, tm = table.shape
    t = h.shape[0]
    wspec = lambda s: pl.BlockSpec((1,) + s, lambda i, te, nv: (te[i], 0, 0))
    return pl.pallas_call(
        _moe_ffn_kernel,
        grid_spec=pltpu.PrefetchScalarGridSpec(
            num_scalar_prefetch=2,
            grid=(n_tiles,),
            in_specs=[pl.BlockSpec((1, 1, tm), lambda i, te, nv: (i, 0, 0), memory_space=pltpu.SMEM),
                      pl.BlockSpec(memory_space=pl.ANY),
                      wspec((D_MODEL, D_FF)), wspec((D_MODEL, D_FF)), wspec((D_FF, D_MODEL))],
            out_specs=pl.BlockSpec(memory_space=pl.ANY),
            scratch_shapes=[pltpu.VMEM((tm, D_MODEL), F32), pltpu.VMEM((tm, D_MODEL), F32),
                            pltpu.VMEM((tm, D_FF), BF16), pltpu.SemaphoreType.DMA(()), pltpu.SemaphoreType.DMA(())],
        ),
        out_shape=jax.ShapeDtypeStruct((2 * t, D_MODEL), F32),
        compiler_params=_params("arbitrary"),
        name="moe_ffn",
    )(texp, nvalid, table, h, wg, wu, wd)


def _moe_combine_kernel(x_ref, y_ref, g_ref, o_ref):
    g = g_ref[...]
    o_ref[...] = x_ref[...] + (g[:, 0:1] * y_ref[:, :D_MODEL] + g[:, 1:2] * y_ref[:, D_MODEL:])


def _moe_combine(x, y2, gates):
    t = x.shape[0]
    tm = min(TOKEN_TILE, t)
    row = lambda n: pl.BlockSpec((tm, n), lambda i: (i, 0))
    return pl.pallas_call(
        _moe_combine_kernel,
        grid=(t // tm,),
        in_specs=[row(D_MODEL), row(2 * D_MODEL), row(2)],
        out_specs=row(D_MODEL),
        out_shape=jax.ShapeDtypeStruct((t, D_MODEL), F32),
        compiler_params=_params("arbitrary"),
        name="moe_combine",
    )(x, y2, gates)


def _moe(x, gain, router, wg, wu, wd):
    t = x.shape[0]
    tm = TOKEN_TILE if 2 * t >= N_EXPERTS * 4 * TOKEN_TILE else TOKEN_TILE // 2
    h, comb, mask = _moe_route(x, gain, router)
    texp, nvalid, table, gates = _moe_plan(mask, comb, tm)
    y = _moe_ffn(texp, nvalid, table, h, wg, wu, wd)
    return _moe_combine(x, y.reshape(t, 2 * D_MODEL), gates)


def _pair_states(s):
    b = s.shape[0]
    s = s.reshape(b, RWKV_HEADS // 2, 2, RWKV_HEAD, RWKV_HEAD)
    z = jnp.zeros_like(s[:, :, 0])
    top = jnp.concatenate([s[:, :, 0], z], axis=-1)
    bot = jnp.concatenate([z, s[:, :, 1]], axis=-1)
    return jnp.concatenate([top, bot], axis=-2)


def _unpair_states(s):
    b = s.shape[0]
    s0 = s[:, :, :RWKV_HEAD, :RWKV_HEAD]
    s1 = s[:, :, RWKV_HEAD:, RWKV_HEAD:]
    return jnp.stack([s0, s1], axis=2).reshape(b, RWKV_HEADS, RWKV_HEAD, RWKV_HEAD)


def _trunk(x, ssm, conv, ck, cv, wkv, shift, w):
    b, l, _ = x.shape
    t = b * l
    ones_blk = w["ones_blk"]
    xf = x.reshape(t, D_MODEL)

    z, xbc, dt, q, k, v = _inproj(xf, w["norm_mix0"], w["wz"], w["wxbc"], w["wdt"], w["wq"], w["wk"], w["wv"],
                                  w["q_norm"], w["k_norm"], ones_blk)
    lp = l if l % SSD_CHUNK == 0 else 8
    seq = lambda a: a.reshape(b, l, a.shape[-1])
    padl = lambda a: a if lp == l else jnp.pad(a, ((0, 0), (0, lp - l), (0, 0)))
    y_ssd, ssm1, conv1 = _ssd(padl(seq(xbc)), padl(seq(dt)), padl(seq(z)), conv, ssm.reshape(b, SSD_WIDTH, SSD_STATE),
                              w["conv_w"], w["conv_b"], w["dt_bias"], w["a_log"], w["d_skip"], w["ssd_norm"], lv=l)
    y_ssd = y_ssd[:, :l].reshape(t, SSD_WIDTH)
    ssm1 = ssm1.reshape(b, SSD_HEADS, SSD_HEAD_DIM, SSD_STATE)
    if ck is None:
        o = _swa_prompt(seq(q), seq(k), seq(v), w["sinks"]).reshape(t, Q_DIM)
        k1 = seq(k)[:, -WINDOW:].reshape(b, WINDOW, ATT_KV_HEADS, ATT_HEAD_DIM)
        v1 = seq(v)[:, -WINDOW:].reshape(b, WINDOW, ATT_KV_HEADS, ATT_HEAD_DIM)
    else:
        wc = ck.shape[1]
        o, k1, v1 = _swa_cached(q, k, v, ck.reshape(b, wc, KV_DIM), cv.reshape(b, wc, KV_DIM), w["sinks"], l=l)
        k1 = k1.reshape(b, wc, ATT_KV_HEADS, ATT_HEAD_DIM)
        v1 = v1.reshape(b, wc, ATT_KV_HEADS, ATT_HEAD_DIM)
    x2 = _outproj_ffn(xf, y_ssd, o, w["wo_a"], w["wo_b"], w["norm_ffn0"], w["ffn_gate"], w["ffn_up"], w["ffn_down"])

    tm = min(TOKEN_TILE, t)
    if l >= tm:
        first = shift.reshape(b, 1, D_MODEL)
    else:
        first = jnp.repeat(shift, l, axis=0)
    r, lw, kx, vx, an, bb, gate, h1 = _rwkv_prep(
        x2, first, w["norm_mix1"], w["mu"], w["w0"], w["a0"], w["kk"], w["ka"], w["wr"], w["wkk"], w["wvv"],
        w["w1"], w["w2"], w["a1"], w["a2"], w["g1"], w["g2"], ones_blk, seq_len=l)
    shift1 = h1.reshape(b, l, D_MODEL)[:, -1]
    lpr = -(-l // RWKV_CHUNK) * RWKV_CHUNK
    padr = lambda a: seq(a) if lpr == l else jnp.pad(seq(a), ((0, 0), (0, lpr - l), (0, 0)))
    y, s1 = _rwkv_scan(padr(r), padr(lw), padr(kx), padr(vx), padr(an), padr(bb), _pair_states(wkv))
    y = y[:, :l].reshape(t, D_MODEL)
    x3 = _rwkv_out(x2, y, r, kx, vx, gate, w["ln_w"], w["ln_b"], w["rk"], w["wo"], ones_blk)
    x4 = _moe(x3, w["norm_ffn1"], w["router"], w["moe_gate"], w["moe_up"], w["moe_down"])
    return (x4.reshape(b, l, D_MODEL), ssm1[None], conv1[None], k1[None], v1[None],
            _unpair_states(s1)[None], shift1[None])


def kernel(x_prompt, x_sample, state_ssm, state_conv, cache_swa_k, cache_swa_v, state_wkv, state_shift, norm_mix, norm_ffn, w_in, conv_w, conv_b, dt_bias, a_log, d_skip, ssd_norm, q_norm, k_norm, attn_sinks, w_out, ffn_gate, ffn_up, ffn_down, rwkv_mu, rwkv_w0, rwkv_w1, rwkv_w2, rwkv_a0, rwkv_a1, rwkv_a2, rwkv_g1, rwkv_g2, rwkv_kk, rwkv_ka, rwkv_rk, rwkv_wr, rwkv_wk, rwkv_wv, rwkv_wo, rwkv_ln_w, rwkv_ln_b, moe_router, moe_gate, moe_up, moe_down):
    bf = lambda a: a.astype(BF16)
    row = lambda a: a.reshape(1, -1).astype(F32)
    padlane = lambda a: jnp.pad(a, ((0, 0), (0, LANES - a.shape[1])))
    wi = w_in[0]
    c0 = SSD_WIDTH
    c1 = c0 + CONV_DIM
    c2 = c1 + SSD_HEADS
    c3 = c2 + Q_DIM
    c4 = c3 + KV_DIM
    w = dict(
        ones_blk=_block_ones(),
        norm_mix0=row(norm_mix[0]), norm_mix1=row(norm_mix[1]), norm_ffn0=row(norm_ffn[0]), norm_ffn1=row(norm_ffn[1]),
        wz=bf(wi[:, :c0]), wxbc=bf(wi[:, c0:c1]), wdt=bf(padlane(wi[:, c1:c2])), wq=bf(wi[:, c2:c3]),
        wk=bf(wi[:, c3:c4]), wv=bf(wi[:, c4:]),
        q_norm=row(jnp.tile(q_norm[0], ATT_HEADS)), k_norm=row(jnp.tile(k_norm[0], ATT_KV_HEADS)),
        conv_w=conv_w[0], conv_b=row(conv_b[0]), dt_bias=padlane(row(dt_bias[0])), a_log=padlane(row(a_log[0])),
        d_skip=row(jnp.repeat(d_skip[0], SSD_HEAD_DIM)), ssd_norm=row(ssd_norm[0]), sinks=attn_sinks[0].astype(F32),
        wo_a=bf(w_out[0, :SSD_WIDTH]), wo_b=bf(w_out[0, SSD_WIDTH:]),
        ffn_gate=bf(ffn_gate[0]), ffn_up=bf(ffn_up[0]), ffn_down=bf(ffn_down[0]),
        mu=rwkv_mu[0], w0=row(rwkv_w0[0]), a0=row(rwkv_a0[0]), kk=row(rwkv_kk[0]), ka=row(rwkv_ka[0]),
        wr=bf(rwkv_wr[0]), wkk=bf(rwkv_wk[0]), wvv=bf(rwkv_wv[0]), wo=bf(rwkv_wo[0]),
        w1=bf(rwkv_w1[0]), w2=bf(rwkv_w2[0]), a1=bf(rwkv_a1[0]), a2=bf(rwkv_a2[0]),
        g1=bf(rwkv_g1[0]), g2=bf(rwkv_g2[0]),
        ln_w=row(rwkv_ln_w[0]), ln_b=row(rwkv_ln_b[0]), rk=row(rwkv_rk[0]),
        router=padlane(moe_router[0]), moe_gate=bf(moe_gate[0]), moe_up=bf(moe_up[0]), moe_down=bf(moe_down[0]),
    )
    bp = x_prompt.shape[0]
    z_ssm = jnp.zeros((bp,) + state_ssm.shape[2:], F32)
    z_conv = jnp.zeros((bp,) + state_conv.shape[2:], F32)
    z_wkv = jnp.zeros((bp,) + state_wkv.shape[2:], F32)
    z_shift = jnp.zeros((bp,) + state_shift.shape[2:], F32)
    outs_p = _trunk(x_prompt, z_ssm, z_conv, None, None, z_wkv, z_shift, w)
    outs_s = _trunk(x_sample, state_ssm[0], state_conv[0], cache_swa_k[0], cache_swa_v[0], state_wkv[0],
                    state_shift[0], w)
    return (outs_p[0], outs_s[0]) + tuple(outs_p[1:]) + tuple(outs_s[1:])
```

```python
import functools

import jax
import jax.numpy as jnp
from jax import lax
from jax.experimental import pallas as pl
from jax.experimental.pallas import tpu as pltpu

F32 = jnp.float32
BF16 = jnp.bfloat16

D_MODEL = 1024
SSD_HEAD_DIM = 64
SSD_HEADS = 16
SSD_GROUPS = 2
SSD_STATE = 128
SSD_CONV = 4
SSD_CHUNK = 128
SSD_WIDTH = 1024
CONV_DIM = SSD_WIDTH + 2 * SSD_GROUPS * SSD_STATE
ATT_HEAD_DIM = 64
ATT_HEADS = 16
ATT_KV_HEADS = 4
ATT_REP = ATT_HEADS // ATT_KV_HEADS
WINDOW = 128
Q_DIM = ATT_HEADS * ATT_HEAD_DIM
KV_DIM = ATT_KV_HEADS * ATT_HEAD_DIM
RWKV_HEAD = 64
RWKV_HEADS = 16
RWKV_GN_EPS = 64e-5
RWKV_CHUNK = 64
D_FF = 2816
N_EXPERTS = 8
RMS_EPS = 1e-6
NEG_INF = -1e30

LANES = 128
MXU_DIM = 256
VMEM_LIMIT_BYTES = 56 * 1024 * 1024
TOKEN_TILE = 512
FF_CHUNK = MXU_DIM
HIGHEST = lax.Precision.HIGHEST


def _dot(a, b, precision=None):
    return jnp.dot(a, b, preferred_element_type=F32, precision=precision)


def _dot_nt(a, b, precision=None):
    return lax.dot_general(a, b, (((1,), (1,)), ((), ())), preferred_element_type=F32, precision=precision)


def _dot_tn(a, b, precision=None):
    return lax.dot_general(a, b, (((0,), (0,)), ((), ())), preferred_element_type=F32, precision=precision)


def _rms_rows(x, g):
    return x * lax.rsqrt(jnp.mean(x * x, axis=-1, keepdims=True) + RMS_EPS) * g


def _sigmoid(x):
    return 1.0 / (1.0 + jnp.exp(-x))


def _silu(x):
    return x * _sigmoid(x)


def _softplus(x):
    return jnp.maximum(x, 0.0) + jnp.log(1.0 + jnp.exp(-jnp.abs(x)))


def _const_spec(shape):
    zeros = (0,) * len(shape)
    return pl.BlockSpec(shape, lambda *_: zeros, pipeline_mode=pl.Buffered(1))


def _params(*semantics):
    return pltpu.CompilerParams(dimension_semantics=semantics, vmem_limit_bytes=VMEM_LIMIT_BYTES)


def _head_sums(t, ones_blk):
    return _dot(t.astype(BF16), ones_blk)


def _block_ones():
    r = jnp.arange(MXU_DIM) // ATT_HEAD_DIM
    return (r[:, None] == r[None, :]).astype(BF16)


def _inproj_kernel(x_ref, g_ref, wz_ref, wxbc_ref, wdt_ref, wq_ref, wk_ref, wv_ref, qn_ref, kn_ref, ones_ref,
                   z_ref, xbc_ref, dt_ref, q_ref, k_ref, v_ref):
    h = _rms_rows(x_ref[...], g_ref[...]).astype(BF16)
    ones_blk = ones_ref[...]

    def head_rms(t, gain):
        outs = []
        for c in range(t.shape[1] // MXU_DIM):
            tc = t[:, c * MXU_DIM:(c + 1) * MXU_DIM]
            ms = _head_sums(tc * tc, ones_blk) * (1.0 / ATT_HEAD_DIM)
            outs.append(tc * lax.rsqrt(ms + RMS_EPS))
        return jnp.concatenate(outs, axis=1) * gain

    z_ref[...] = _dot(h, wz_ref[...]).astype(z_ref.dtype)
    xbc_ref[...] = _dot(h, wxbc_ref[...])
    dt_ref[...] = _dot(h, wdt_ref[...])
    q_ref[...] = head_rms(_dot(h, wq_ref[...]), qn_ref[...]).astype(q_ref.dtype)
    k_ref[...] = head_rms(_dot(h, wk_ref[...]), kn_ref[...])
    v_ref[...] = _dot(h, wv_ref[...])


def _inproj(x, gain, wz, wxbc, wdt, wq, wk, wv, qn, kn, ones_blk):
    t = x.shape[0]
    tm = min(TOKEN_TILE, t)
    row = lambda n: pl.BlockSpec((tm, n), lambda i: (i, 0))
    return pl.pallas_call(
        _inproj_kernel,
        grid=(t // tm,),
        in_specs=[row(D_MODEL), _const_spec((1, D_MODEL)),
                  _const_spec(wz.shape), _const_spec(wxbc.shape), _const_spec(wdt.shape),
                  _const_spec(wq.shape), _const_spec(wk.shape), _const_spec(wv.shape),
                  _const_spec((1, Q_DIM)), _const_spec((1, KV_DIM)), _const_spec((MXU_DIM, MXU_DIM))],
        out_specs=[row(SSD_WIDTH), row(CONV_DIM), row(LANES), row(Q_DIM), row(KV_DIM), row(KV_DIM)],
        out_shape=[jax.ShapeDtypeStruct((t, SSD_WIDTH), BF16), jax.ShapeDtypeStruct((t, CONV_DIM), F32),
                   jax.ShapeDtypeStruct((t, LANES), F32), jax.ShapeDtypeStruct((t, Q_DIM), BF16),
                   jax.ShapeDtypeStruct((t, KV_DIM), F32), jax.ShapeDtypeStruct((t, KV_DIM), F32)],
        compiler_params=_params("arbitrary"),
        name="inproj",
    )(x, gain, wz, wxbc, wdt, wq, wk, wv, qn, kn, ones_blk)


def _ssd_kernel(xbc_ref, dt_ref, z_ref, conv0_ref, ssm0_ref, cw_ref, cb_ref, dtb_ref, alog_ref, dsk_ref, nrm_ref,
                y_ref, ssm1_ref, conv1_ref, xpad_ref, dtpad_ref, s_ref, *, q, lin, lv, nchunks):
    c = pl.program_id(1)
    n_pairs = SSD_HEADS // 2
    hist = SSD_CONV - 1
    base = 8 - hist

    @pl.when(c == 0)
    def _():
        xpad_ref[...] = jnp.zeros(xpad_ref.shape, F32)
        dtpad_ref[...] = jnp.zeros(dtpad_ref.shape, F32)
        xpad_ref[base:8, :] = conv0_ref[0]
        for jb in range(n_pairs):
            s_ref[:, jb * LANES:(jb + 1) * LANES] = ssm0_ref[0, jb * LANES:(jb + 1) * LANES, :].T

    xpad_ref[8:8 + lin, :] = xbc_ref[0]
    dtpad_ref[0:lin, :] = dt_ref[0]

    conv = cb_ref[...]
    for j in range(SSD_CONV):
        conv = conv + xpad_ref[base + j:base + j + q, :] * cw_ref[j:j + 1, :]
    tail = xpad_ref[base + lv:8 + lv, :]
    conv1_ref[0] = tail
    xpad_ref[base:8, :] = tail

    xc = _silu(conv)
    xs = xc[:, :SSD_WIDTH]
    bm = xc[:, SSD_WIDTH:SSD_WIDTH + SSD_GROUPS * SSD_STATE].astype(BF16)
    cm = xc[:, SSD_WIDTH + SSD_GROUPS * SSD_STATE:].astype(BF16)

    row_i = lax.broadcasted_iota(jnp.int32, (q, LANES), 0)
    col_i = lax.broadcasted_iota(jnp.int32, (q, LANES), 1)
    first_half = col_i < SSD_HEAD_DIM
    dt = jnp.where(row_i < lv, _softplus(dtpad_ref[...] + dtb_ref[...]), 0.0)
    a = dt * (-jnp.exp(alog_ref[...]))
    r_q = lax.broadcasted_iota(jnp.int32, (q, q), 0)
    c_q = lax.broadcasted_iota(jnp.int32, (q, q), 1)
    causal = r_q >= c_q
    a_cs = _dot(causal.astype(F32), a, precision=HIGHEST)
    a_cs_t = a_cs.T
    ea = jnp.exp(a_cs)
    dte = jnp.exp(a_cs[q - 1:q, :] - a_cs)

    def colb(arr, h):
        return jnp.broadcast_to(arr[:, h:h + 1], (q, LANES))

    def pairb(arr, h0):
        return jnp.where(first_half, colb(arr, h0), colb(arr, h0 + 1))

    cbs = []
    for g in range(SSD_GROUPS):
        sl = slice(g * SSD_STATE, (g + 1) * SSD_STATE)
        cbs.append(_dot_nt(cm[:, sl], bm[:, sl]))

    ys = []
    for j in range(n_pairs):
        g = (2 * j) // (SSD_HEADS // SSD_GROUPS)
        gsl = slice(g * SSD_STATE, (g + 1) * SSD_STATE)
        psl = slice(j * LANES, (j + 1) * LANES)
        xs_p = xs[:, psl]
        xdt = xs_p * pairb(dt, 2 * j)
        xdt_b = xdt.astype(BF16)
        yd = []
        for hh in (2 * j, 2 * j + 1):
            diff = colb(a_cs, hh) - a_cs_t[hh:hh + 1, :]
            lmat = jnp.exp(jnp.where(causal, diff, NEG_INF))
            yd.append(_dot((cbs[g] * lmat).astype(BF16), xdt_b))
        y_diag = jnp.where(first_half, yd[0], yd[1])
        ea_p = pairb(ea, 2 * j)
        s_old = s_ref[:, psl]
        y_off = _dot(cm[:, gsl], s_old.astype(BF16)) * ea_p
        s_ref[:, psl] = s_old * ea_p[q - 1:q, :] + _dot_tn(bm[:, gsl], (xdt * pairb(dte, 2 * j)).astype(BF16))
        ys.append(y_diag + y_off + xs_p * dsk_ref[:, psl])
    y = jnp.concatenate(ys, axis=1)[0:lin]

    y = y * _silu(z_ref[0].astype(F32))
    gw = SSD_WIDTH // SSD_GROUPS
    outs = []
    for g in range(SSD_GROUPS):
        yg = y[:, g * gw:(g + 1) * gw]
        outs.append(yg * lax.rsqrt(jnp.mean(yg * yg, axis=-1, keepdims=True) + RMS_EPS))
    y_ref[0] = (jnp.concatenate(outs, axis=1) * nrm_ref[...]).astype(y_ref.dtype)

    @pl.when(c == nchunks - 1)
    def _():
        for jb in range(n_pairs):
            ssm1_ref[0, jb * LANES:(jb + 1) * LANES, :] = s_ref[:, jb * LANES:(jb + 1) * LANES].T


def _ssd(xbc, dt, z, conv0, ssm0, cw, cb, dtb, alog, dsk, nrm, *, lv):
    b, lp, _ = xbc.shape
    q = SSD_CHUNK
    lin = min(q, lp)
    nchunks = lp // lin
    lv_chunk = min(lv, lin)
    kern = functools.partial(_ssd_kernel, q=q, lin=lin, lv=lv_chunk, nchunks=nchunks)
    seq = lambda n: pl.BlockSpec((1, lin, n), lambda i, c: (i, c, 0))
    per_b = lambda s: pl.BlockSpec((1,) + s, lambda i, c: (i, 0, 0))
    return pl.pallas_call(
        kern,
        grid=(b, nchunks),
        in_specs=[seq(CONV_DIM), seq(LANES), seq(SSD_WIDTH),
                  per_b((SSD_CONV - 1, CONV_DIM)), per_b((SSD_WIDTH, SSD_STATE)),
                  _const_spec((SSD_CONV, CONV_DIM)), _const_spec((1, CONV_DIM)), _const_spec((1, LANES)),
                  _const_spec((1, LANES)), _const_spec((1, SSD_WIDTH)), _const_spec((1, SSD_WIDTH))],
        out_specs=[seq(SSD_WIDTH), per_b((SSD_WIDTH, SSD_STATE)), per_b((SSD_CONV - 1, CONV_DIM))],
        out_shape=[jax.ShapeDtypeStruct((b, lp, SSD_WIDTH), BF16),
                   jax.ShapeDtypeStruct((b, SSD_WIDTH, SSD_STATE), F32),
                   jax.ShapeDtypeStruct((b, SSD_CONV - 1, CONV_DIM), F32)],
        scratch_shapes=[pltpu.VMEM((q + 8, CONV_DIM), F32), pltpu.VMEM((q, LANES), F32),
                        pltpu.VMEM((SSD_STATE, SSD_WIDTH), F32)],
        compiler_params=_params("arbitrary", "arbitrary"),
        name="ssd",
    )(xbc, dt, z, conv0, ssm0, cw, cb, dtb, alog, dsk, nrm)


def _sink_softmax(s, allowed, sink_col):
    s = jnp.where(allowed, s, NEG_INF)
    m = jnp.maximum(jnp.max(s, axis=-1, keepdims=True), sink_col)
    p = jnp.exp(s - m)
    denom = jnp.sum(p, axis=-1, keepdims=True) + jnp.exp(sink_col - m)
    return p.astype(BF16), 1.0 / denom


def _gqa_blocks(blocks, sink_ref, allowed, nq):
    d = ATT_HEAD_DIM
    heads = range(ATT_KV_HEADS)
    sinks = [jnp.concatenate([jnp.full((nq, 1), sink_ref[ATT_REP * j + r], F32) for r in range(ATT_REP)], axis=0)
             for j in heads]
    scores = []
    for qb, kk, _ in blocks:
        for j in heads:
            qj = jnp.concatenate(
                [qb[:, (ATT_REP * j + r) * d:(ATT_REP * j + r + 1) * d] for r in range(ATT_REP)], axis=0)
            scores.append(_dot_nt(qj * (d ** -0.5), kk[:, j * d:(j + 1) * d]))
    probs = [_sink_softmax(s, allowed, sinks[n % ATT_KV_HEADS]) for n, s in enumerate(scores)]
    results = []
    for bi, (_, _, vv) in enumerate(blocks):
        outs = []
        for j in heads:
            p, inv = probs[bi * ATT_KV_HEADS + j]
            o = _dot(p, vv[:, j * d:(j + 1) * d]) * inv
            outs.extend(o[r * nq:(r + 1) * nq] for r in range(ATT_REP))
        results.append(jnp.concatenate(outs, axis=1))
    return results


def _swa_prompt_kernel(sink_ref, q_ref, kp_ref, kc_ref, vp_ref, vc_ref, o_ref):
    i = pl.program_id(1)
    w = WINDOW
    kk = jnp.concatenate([kp_ref[0], kc_ref[0]], axis=0).astype(BF16)
    vv = jnp.concatenate([vp_ref[0], vc_ref[0]], axis=0).astype(BF16)
    rows = lax.broadcasted_iota(jnp.int32, (ATT_REP * w, 2 * w), 0) % w
    cols = lax.broadcasted_iota(jnp.int32, (ATT_REP * w, 2 * w), 1)
    rel = rows + w - cols
    allowed = (rel >= 0) & (rel <= w) & ((cols >= w) | (i > 0))
    o_ref[0] = _gqa_blocks([(q_ref[0], kk, vv)], sink_ref, allowed, w)[0].astype(o_ref.dtype)


def _swa_prompt(q, k, v, sinks):
    b, l, _ = q.shape
    nb = l // WINDOW
    cur = lambda n: pl.BlockSpec((1, WINDOW, n), lambda i, c: (i, c, 0))
    prev = lambda n: pl.BlockSpec((1, WINDOW, n), lambda i, c: (i, jnp.maximum(c - 1, 0), 0))
    return pl.pallas_call(
        _swa_prompt_kernel,
        grid=(b, nb),
        in_specs=[pl.BlockSpec(memory_space=pltpu.SMEM),
                  cur(Q_DIM), prev(KV_DIM), cur(KV_DIM), prev(KV_DIM), cur(KV_DIM)],
        out_specs=cur(Q_DIM),
        out_shape=jax.ShapeDtypeStruct((b, l, Q_DIM), BF16),
        compiler_params=_params("arbitrary", "arbitrary"),
        name="swa_prompt",
    )(sinks, q, k, k, v, v)


def _swa_cached_kernel(sink_ref, q_ref, k_ref, v_ref, kbuf_ref, vbuf_ref, o_ref, k1_ref, v1_ref, *, nbatch, l):
    wc = kbuf_ref.shape[1]
    d = ATT_HEAD_DIM
    pad = 8 - l
    nk = wc + 8
    rows = lax.broadcasted_iota(jnp.int32, (ATT_REP * l, nk), 0) % l
    cols = lax.broadcasted_iota(jnp.int32, (ATT_REP * l, nk), 1)
    rel = rows + wc - cols
    allowed = (rel >= 0) & (rel <= WINDOW)
    zpad = jnp.zeros((pad, KV_DIM), F32)
    qall = q_ref[...].astype(F32)
    blocks = []
    for bi in range(nbatch):
        kn = k_ref[bi * l:(bi + 1) * l, :]
        vn = v_ref[bi * l:(bi + 1) * l, :]
        kc = jnp.concatenate([kbuf_ref[bi], kn, zpad], axis=0).astype(BF16)
        vc = jnp.concatenate([vbuf_ref[bi], vn, zpad], axis=0).astype(BF16)
        blocks.append((qall[bi * l:(bi + 1) * l, :].astype(BF16), kc, vc))
        k1_ref[bi, 0:wc - l, :] = kbuf_ref[bi, l:wc, :]
        k1_ref[bi, wc - l:wc, :] = kn
        v1_ref[bi, 0:wc - l, :] = vbuf_ref[bi, l:wc, :]
        v1_ref[bi, wc - l:wc, :] = vn
    o_ref[...] = jnp.concatenate(_gqa_blocks(blocks, sink_ref, allowed, l), axis=0).astype(o_ref.dtype)


def _swa_cached(q, k, v, kbuf, vbuf, sinks, *, l):
    b, wc, _ = kbuf.shape
    nbatch = 8
    kern = functools.partial(_swa_cached_kernel, nbatch=nbatch, l=l)
    rows = lambda n: pl.BlockSpec((nbatch * l, n), lambda i: (i, 0))
    cache = pl.BlockSpec((nbatch, wc, KV_DIM), lambda i: (i, 0, 0))
    return pl.pallas_call(
        kern,
        grid=(b // nbatch,),
        in_specs=[pl.BlockSpec(memory_space=pltpu.SMEM), rows(Q_DIM), rows(KV_DIM), rows(KV_DIM), cache, cache],
        out_specs=[rows(Q_DIM), cache, cache],
        out_shape=[jax.ShapeDtypeStruct((b * l, Q_DIM), BF16),
                   jax.ShapeDtypeStruct((b, wc, KV_DIM), F32), jax.ShapeDtypeStruct((b, wc, KV_DIM), F32)],
        compiler_params=_params("arbitrary"),
        name="swa_cached",
    )(sinks, q, k, v, kbuf, vbuf)


def _swiglu_act(h, wg_ref, wu_ref, act_ref):
    for c in range(D_FF // FF_CHUNK):
        sl = slice(c * FF_CHUNK, (c + 1) * FF_CHUNK)
        act_ref[:, sl] = (_silu(_dot(h, wg_ref[:, sl])) * _dot(h, wu_ref[:, sl])).astype(BF16)


def _outproj_ffn_kernel(x_ref, ya_ref, yb_ref, woa_ref, wob_ref, g_ref, wg_ref, wu_ref, wd_ref, o_ref, act_ref):
    x1 = x_ref[...] + _dot(ya_ref[...], woa_ref[...]) + _dot(yb_ref[...], wob_ref[...])
    h = _rms_rows(x1, g_ref[...]).astype(BF16)
    _swiglu_act(h, wg_ref, wu_ref, act_ref)
    o_ref[...] = x1 + _dot(act_ref[...], wd_ref[...])


def _outproj_ffn(x, ya, yb, woa, wob, gain, wg, wu, wd):
    t = x.shape[0]
    tm = min(TOKEN_TILE, t)
    row = lambda n: pl.BlockSpec((tm, n), lambda i: (i, 0))
    return pl.pallas_call(
        _outproj_ffn_kernel,
        grid=(t // tm,),
        in_specs=[row(D_MODEL), row(SSD_WIDTH), row(Q_DIM), _const_spec(woa.shape), _const_spec(wob.shape),
                  _const_spec((1, D_MODEL)), _const_spec(wg.shape), _const_spec(wu.shape), _const_spec(wd.shape)],
        out_specs=row(D_MODEL),
        out_shape=jax.ShapeDtypeStruct((t, D_MODEL), F32),
        scratch_shapes=[pltpu.VMEM((tm, D_FF), BF16)],
        compiler_params=_params("arbitrary"),
        name="outproj_ffn",
    )(x, ya, yb, woa, wob, gain, wg, wu, wd)


def _rwkv_prep_kernel(x_ref, first_ref, g_ref, mu_ref, w0_ref, a0_ref, kkc_ref, ka_ref,
                      wr_ref, wk_ref, wv_ref, w1_ref, w2_ref, a1_ref, a2_ref, g1_ref, g2_ref, ones_ref,
                      r_ref, lw_ref, k_ref, v_ref, an_ref, b_ref, gate_ref, h_ref, hs_ref, *, tm, seq_len):
    i = pl.program_id(0)
    h = _rms_rows(x_ref[...], g_ref[...])
    tiles_per_seq = max(seq_len // tm, 1)

    if seq_len >= tm:
        h_ref[0] = h[tm - 1:tm, :]

        @pl.when(i % tiles_per_seq == 0)
        def _():
            hs_ref[7:8, :] = first_ref[0]
    else:
        h_ref[...] = h
        hs_ref[0:8, :] = jnp.zeros((8, D_MODEL), F32)
    hs_ref[8:8 + tm, :] = h
    prev = hs_ref[7:7 + tm, :]
    if seq_len < tm:
        rows = lax.broadcasted_iota(jnp.int32, (tm, D_MODEL), 0)
        prev = jnp.where(rows % seq_len == 0, first_ref[...], prev)
    else:
        hs_ref[7:8, :] = h[tm - 1:tm, :]
    xx = prev - h
    mix = lambda n: (h + xx * mu_ref[n:n + 1, :]).astype(BF16)
    r = _dot(mix(0), wr_ref[...])
    k = _dot(mix(2), wk_ref[...])
    v = _dot(mix(3), wv_ref[...])
    w_lora = _dot(jnp.tanh(_dot(mix(1), w1_ref[...])).astype(BF16), w2_ref[...])
    a_lora = _dot(_dot(mix(4), a1_ref[...]).astype(BF16), a2_ref[...])
    gate = _dot(_sigmoid(_dot(mix(5), g1_ref[...])).astype(BF16), g2_ref[...])
    w_log = -_softplus(-(w0_ref[...] + w_lora)) - 0.5
    a_sig = _sigmoid(a0_ref[...] + a_lora)
    kk = k * kkc_ref[...]
    ones_blk = ones_ref[...]
    parts = []
    for c in range(D_MODEL // MXU_DIM):
        kc = kk[:, c * MXU_DIM:(c + 1) * MXU_DIM]
        nrm = jnp.maximum(jnp.sqrt(_head_sums(kc * kc, ones_blk)), 1e-12)
        parts.append(kc / nrm)
    kk = jnp.concatenate(parts, axis=1)
    r_ref[...] = r.astype(r_ref.dtype)
    lw_ref[...] = -jnp.exp(w_log)
    k_ref[...] = (k * (1.0 + (a_sig - 1.0) * ka_ref[...])).astype(k_ref.dtype)
    v_ref[...] = v.astype(v_ref.dtype)
    an_ref[...] = (-kk).astype(an_ref.dtype)
    b_ref[...] = (kk * a_sig).astype(b_ref.dtype)
    gate_ref[...] = gate.astype(gate_ref.dtype)


def _rwkv_prep(x, first, gain, mu, w0, a0, kkc, ka, wr, wk, wv, w1, w2, a1, a2, g1, g2, ones_blk, *, seq_len):
    t = x.shape[0]
    tm = min(TOKEN_TILE, t)
    kern = functools.partial(_rwkv_prep_kernel, tm=tm, seq_len=seq_len)
    row = lambda n: pl.BlockSpec((tm, n), lambda i: (i, 0))
    if seq_len >= tm:
        tiles_per_seq = seq_len // tm
        first_spec = pl.BlockSpec((1, 1, D_MODEL), lambda i: (i // tiles_per_seq, 0, 0))
        last_spec = first_spec
        last_shape = jax.ShapeDtypeStruct((t // seq_len, 1, D_MODEL), F32)
    else:
        first_spec = row(D_MODEL)
        last_spec = row(D_MODEL)
        last_shape = jax.ShapeDtypeStruct((t, D_MODEL), F32)
    vec = _const_spec((1, D_MODEL))
    wide = lambda dt: jax.ShapeDtypeStruct((t, D_MODEL), dt)
    outs = [wide(BF16), wide(F32), wide(BF16), wide(BF16), wide(BF16), wide(BF16), wide(BF16), last_shape]
    return pl.pallas_call(
        kern,
        grid=(t // tm,),
        in_specs=[row(D_MODEL), first_spec, vec, _const_spec((6, D_MODEL)), vec, vec, vec, vec,
                  _const_spec(wr.shape), _const_spec(wk.shape), _const_spec(wv.shape),
                  _const_spec(w1.shape), _const_spec(w2.shape), _const_spec(a1.shape), _const_spec(a2.shape),
                  _const_spec(g1.shape), _const_spec(g2.shape), _const_spec((MXU_DIM, MXU_DIM))],
        out_specs=[row(D_MODEL)] * 7 + [last_spec],
        out_shape=outs,
        scratch_shapes=[pltpu.VMEM((tm + 8, D_MODEL), F32)],
        compiler_params=_params("arbitrary"),
        name="rwkv_prep",
    )(x, first, gain, mu, w0, a0, kkc, ka, wr, wk, wv, w1, w2, a1, a2, g1, g2, ones_blk)


def _rwkv_scan_kernel(r_ref, lw_ref, k_ref, v_ref, a_ref, b_ref, s0_ref, y_ref, s1_ref, s_ref, *, nsub, nsteps):
    step = pl.program_id(1)
    c = RWKV_CHUNK
    n_pairs = RWKV_HEADS // 2
    two_c = 2 * c

    @pl.when(step == 0)
    def _():
        s_ref[...] = s0_ref[0]

    lane = lax.broadcasted_iota(jnp.int32, (c, LANES), 1)
    m_a = lane < RWKV_HEAD
    r_i = lax.broadcasted_iota(jnp.int32, (two_c, two_c), 0)
    c_i = lax.broadcasted_iota(jnp.int32, (two_c, two_c), 1)
    strict = r_i > c_i
    blk = (r_i < c) == (c_i < c)
    r_i2 = lax.broadcasted_iota(jnp.int32, (two_c, 2 * two_c), 0)
    c_i2 = lax.broadcasted_iota(jnp.int32, (two_c, 2 * two_c), 1) % two_c
    incl2 = r_i2 >= c_i2
    t_r = lax.broadcasted_iota(jnp.int32, (c, c), 0)
    t_c = lax.broadcasted_iota(jnp.int32, (c, c), 1)
    tri = (t_r >= t_c).astype(F32)

    def stack(x):
        return jnp.concatenate([jnp.where(m_a, x, 0.0), jnp.where(m_a, 0.0, x)], axis=0)

    def sub_chunk(ci, carry):
        rows = pl.ds(pl.multiple_of(ci * c, c), c)
        lw = lw_ref[0, rows, :]
        cl = _dot(tri, lw, precision=HIGHEST)
        cl_end = cl[c - 1:c, :]
        e_neg = jnp.exp(-cl)
        e_end = jnp.exp(cl_end - cl)
        w_end = jnp.exp(cl_end)
        aa = a_ref[0, rows, :].astype(F32) * jnp.exp(cl - lw)
        rr = r_ref[0, rows, :].astype(F32) * jnp.exp(cl)
        bb = b_ref[0, rows, :].astype(F32)
        kk = k_ref[0, rows, :].astype(F32)
        vv = v_ref[0, rows, :].astype(F32)
        pairs = range(n_pairs)
        sl = [slice(p * LANES, (p + 1) * LANES) for p in pairs]
        lhs = [jnp.concatenate([stack(aa[:, sl[p]]), stack(rr[:, sl[p]])], axis=0).astype(BF16) for p in pairs]
        rhs = [jnp.concatenate([stack((bb * e_neg)[:, sl[p]]), stack((kk * e_neg)[:, sl[p]])], axis=0).astype(BF16)
               for p in pairs]
        bk = [jnp.concatenate([stack((bb * e_end)[:, sl[p]]), stack((kk * e_end)[:, sl[p]])], axis=0).astype(BF16)
              for p in pairs]
        v_s = [stack(vv[:, sl[p]]) for p in pairs]
        s_old = [s_ref[p] for p in pairs]
        gmat = [_dot_nt(lhs[p], rhs[p]) for p in pairs]
        ph = [_dot_nt(lhs[p], s_old[p].astype(BF16)) for p in pairs]
        npow = [jnp.where(strict, gmat[p][:two_c, :two_c], 0.0).astype(BF16) for p in pairs]
        a_ak = [jnp.where(strict, gmat[p][:two_c, two_c:], 0.0).astype(BF16) for p in pairs]
        a_r = [jnp.where(incl2, gmat[p][two_c:, :], 0.0).astype(BF16) for p in pairs]
        u = [ph[p][:two_c] + _dot(a_ak[p], v_s[p].astype(BF16)) for p in pairs]
        for lvl in range(6):
            u = [u[p] + _dot(npow[p], u[p].astype(BF16)) for p in pairs]
            if lvl < 5:
                npow = [_dot(npow[p], npow[p]).astype(BF16) for p in pairs]
        uv = [jnp.concatenate([u[p], v_s[p]], axis=0).astype(BF16) for p in pairs]
        y_s = [ph[p][two_c:] + _dot(a_r[p], uv[p]) for p in pairs]
        y_ref[0, rows, :] = jnp.concatenate([y_s[p][:c] + y_s[p][c:] for p in pairs], axis=1).astype(y_ref.dtype)
        for p in pairs:
            s_new = s_old[p] * w_end[:, sl[p]] + _dot_tn(uv[p], bk[p])
            s_ref[p] = jnp.where(blk, s_new, 0.0)
        return carry

    lax.fori_loop(0, nsub, sub_chunk, 0)

    @pl.when(step == nsteps - 1)
    def _():
        s1_ref[0] = s_ref[...]


def _rwkv_scan(r, lw, k, v, an, b, s0):
    bsz, lp, _ = r.shape
    rows = min(lp, 512)
    nsteps = lp // rows
    nsub = rows // RWKV_CHUNK
    kern = functools.partial(_rwkv_scan_kernel, nsub=nsub, nsteps=nsteps)
    seq = pl.BlockSpec((1, rows, D_MODEL), lambda i, s: (i, s, 0))
    st = pl.BlockSpec((1, RWKV_HEADS // 2, LANES, LANES), lambda i, s: (i, 0, 0, 0))
    return pl.pallas_call(
        kern,
        grid=(bsz, nsteps),
        in_specs=[seq] * 6 + [st],
        out_specs=[seq, st],
        out_shape=[jax.ShapeDtypeStruct((bsz, lp, D_MODEL), BF16),
                   jax.ShapeDtypeStruct((bsz, RWKV_HEADS // 2, LANES, LANES), F32)],
        scratch_shapes=[pltpu.VMEM((RWKV_HEADS // 2, LANES, LANES), F32)],
        compiler_params=_params("arbitrary", "arbitrary"),
        name="rwkv_scan",
    )(r, lw, k, v, an, b, s0)


def _rwkv_out_kernel(x_ref, y_ref, r_ref, k_ref, v_ref, gate_ref, lnw_ref, lnb_ref, rk_ref, wo_ref, ones_ref, o_ref):
    ones_blk = ones_ref[...]
    inv = 1.0 / RWKV_HEAD
    parts = []
    for c in range(D_MODEL // MXU_DIM):
        sl = slice(c * MXU_DIM, (c + 1) * MXU_DIM)
        yh = y_ref[:, sl]
        y = yh.astype(F32)
        yc = y - _dot(yh, ones_blk) * inv
        var = _head_sums(yc * yc, ones_blk) * inv
        yn = yc * lax.rsqrt(var + RWKV_GN_EPS) * lnw_ref[:, sl] + lnb_ref[:, sl]
        rk = r_ref[:, sl].astype(F32) * k_ref[:, sl].astype(F32) * rk_ref[:, sl]
        rkh = rk.astype(BF16)
        bonus = _dot(rkh, ones_blk) + _head_sums(rk - rkh.astype(F32), ones_blk)
        parts.append(yn + bonus * v_ref[:, sl].astype(F32))
    out = jnp.concatenate(parts, axis=1) * gate_ref[...].astype(F32)
    o_ref[...] = x_ref[...] + _dot(out.astype(BF16), wo_ref[...])


def _rwkv_out(x, y, r, k, v, gate, lnw, lnb, rk, wo, ones_blk):
    t = x.shape[0]
    tm = min(TOKEN_TILE, t)
    row = pl.BlockSpec((tm, D_MODEL), lambda i: (i, 0))
    vec = _const_spec((1, D_MODEL))
    return pl.pallas_call(
        _rwkv_out_kernel,
        grid=(t // tm,),
        in_specs=[row] * 6 + [vec, vec, vec, _const_spec(wo.shape), _const_spec((MXU_DIM, MXU_DIM))],
        out_specs=row,
        out_shape=jax.ShapeDtypeStruct((t, D_MODEL), F32),
        compiler_params=_params("arbitrary"),
        name="rwkv_out",
    )(x, y, r, k, v, gate, lnw, lnb, rk, wo, ones_blk)


def _moe_route_kernel(x_ref, g_ref, router_ref, comb_ref, mask_ref):
    h = _rms_rows(x_ref[...], g_ref[...])
    logits = _dot(h, router_ref[...], precision=HIGHEST)
    lane = lax.broadcasted_iota(jnp.int32, logits.shape, 1)
    logits = jnp.where(lane < N_EXPERTS, logits, -jnp.inf)
    m1 = jnp.max(logits, axis=-1, keepdims=True)
    i1 = jnp.min(jnp.where(logits == m1, lane, LANES), axis=-1, keepdims=True)
    rest = jnp.where(lane == i1, -jnp.inf, logits)
    m2 = jnp.max(rest, axis=-1, keepdims=True)
    i2 = jnp.min(jnp.where(rest == m2, lane, LANES), axis=-1, keepdims=True)
    e2 = jnp.exp(m2 - m1)
    g1 = 1.0 / (1.0 + e2)
    g2 = e2 / (1.0 + e2)
    comb_ref[...] = jnp.where(lane == i1, g1, 0.0) + jnp.where(lane == i2, g2, 0.0)
    mask_ref[...] = jnp.where((lane == i1) | (lane == i2), 1.0, 0.0)


def _moe_route(x, gain, router):
    t = x.shape[0]
    tm = min(TOKEN_TILE, t)
    row = lambda n: pl.BlockSpec((tm, n), lambda i: (i, 0))
    return pl.pallas_call(
        _moe_route_kernel,
        grid=(t // tm,),
        in_specs=[row(D_MODEL), _const_spec((1, D_MODEL)), _const_spec((D_MODEL, LANES))],
        out_specs=[row(LANES), row(LANES)],
        out_shape=[jax.ShapeDtypeStruct((t, LANES), F32), jax.ShapeDtypeStruct((t, LANES), F32)],
        compiler_params=_params("arbitrary"),
        name="moe_route",
    )(x, gain, router)


def _moe_plan(mask, comb, tm):
    m = mask[:, :N_EXPERTS].astype(jnp.int32)
    t = m.shape[0]
    tt = min(TOKEN_TILE, t)
    n_tiles = (2 * t) // tm + N_EXPERTS
    rank = jnp.cumsum(m, axis=0)
    counts = rank[-1]
    tiles_e = (counts + tm - 1) // tm
    tile_end = jnp.cumsum(tiles_e)
    tile_start = tile_end - tiles_e
    pos = (tile_start * tm)[None, :] + rank - 1
    slot = jnp.cumsum(m, axis=1) - 1
    c8 = comb[:, :N_EXPERTS]
    pick = [(slot == k) & (m > 0) for k in range(2)]
    pos_k = jnp.stack([jnp.sum(jnp.where(pk, pos, 0), axis=1) for pk in pick], axis=0)
    gates = jnp.stack([jnp.sum(jnp.where(pk, c8, 0.0), axis=1) for pk in pick], axis=1)
    pos_tab = pos_k.reshape(2, t // tt, tt).transpose(1, 0, 2).reshape(t // tt, 1, 2 * tt)
    tile_ids = jnp.arange(n_tiles, dtype=jnp.int32)
    texp = jnp.sum((tile_ids[:, None] >= tile_end[None, :]).astype(jnp.int32), axis=1)
    texp_c = jnp.minimum(texp, N_EXPERTS - 1)
    nvalid = jnp.clip(counts[texp_c] - (tile_ids - tile_start[texp_c]) * tm, 0, tm)
    nvalid = jnp.where(texp < N_EXPERTS, nvalid, 0).astype(jnp.int32)
    last_tile = jnp.concatenate([jnp.where(tiles_e > 0, tile_end - 1, -1), tile_end[-1:]]).astype(jnp.int32)
    return texp_c.astype(jnp.int32), nvalid, last_tile, pos_tab.astype(jnp.int32), gates, n_tiles


def _start_row_copies(pos_ref, tt, make):
    def body(j, c):
        for k in range(2):
            make(k, j, pos_ref[0, 0, k * tt + j]).start()
        return c
    lax.fori_loop(0, tt, body, 0, unroll=8)


def _moe_dispatch_kernel(last_ref, pos_ref, x_ref, g_ref, xs_hbm, hbuf, zbuf, sem, zsem, *, tt, tm, n_tiles):
    def zero_tile(tile):
        return pltpu.make_async_copy(zbuf, xs_hbm.at[pl.ds(pl.multiple_of(tile * tm, tm), tm), :], zsem)

    @pl.when(pl.program_id(0) == 0)
    def _():
        zbuf[...] = jnp.zeros(zbuf.shape, F32)
        for e in range(N_EXPERTS):
            @pl.when(last_ref[e] >= 0)
            def _():
                zero_tile(last_ref[e]).start()
        for e in range(N_EXPERTS):
            @pl.when(last_ref[e] >= 0)
            def _():
                zero_tile(last_ref[e]).wait()

        def clear_unused(tile, c):
            zero_tile(tile).start()
            zero_tile(tile).wait()
            return c
        lax.fori_loop(last_ref[N_EXPERTS], n_tiles, clear_unused, 0)

    hbuf[...] = _rms_rows(x_ref[...], g_ref[...])
    _start_row_copies(pos_ref, tt, lambda k, j, dst: pltpu.make_async_copy(
        hbuf.at[pl.ds(j, 1), :], xs_hbm.at[pl.ds(dst, 1), :], sem))
    for _ in range(2):
        pltpu.make_async_copy(hbuf, xs_hbm.at[pl.ds(0, tt), :], sem).wait()


def _moe_dispatch(last_tile, pos_tab, x, gain, n_rows, tm):
    nsteps, _, tt2 = pos_tab.shape
    tt = tt2 // 2
    kern = functools.partial(_moe_dispatch_kernel, tt=tt, tm=tm, n_tiles=n_rows // tm)
    return pl.pallas_call(
        kern,
        grid_spec=pltpu.PrefetchScalarGridSpec(
            num_scalar_prefetch=1,
            grid=(nsteps,),
            in_specs=[pl.BlockSpec((1, 1, tt2), lambda i, lt: (i, 0, 0), memory_space=pltpu.SMEM),
                      pl.BlockSpec((tt, D_MODEL), lambda i, lt: (i, 0)),
                      pl.BlockSpec((1, D_MODEL), lambda i, lt: (0, 0))],
            out_specs=pl.BlockSpec(memory_space=pl.ANY),
            scratch_shapes=[pltpu.VMEM((tt, D_MODEL), F32), pltpu.VMEM((tm, D_MODEL), F32),
                            pltpu.SemaphoreType.DMA(()), pltpu.SemaphoreType.DMA(())],
        ),
        out_shape=jax.ShapeDtypeStruct((n_rows, D_MODEL), F32),
        compiler_params=_params("arbitrary"),
        name="moe_dispatch",
    )(last_tile, pos_tab, x, gain)


def _moe_ffn_kernel(texp_ref, nvalid_ref, xs_ref, wg_ref, wu_ref, wd_ref, ys_ref, act_ref):
    n = nvalid_ref[pl.program_id(0)]

    @pl.when(n > 0)
    def _():
        _swiglu_act(xs_ref[...].astype(BF16), wg_ref.at[0], wu_ref.at[0], act_ref)
        ys_ref[...] = _dot(act_ref[...], wd_ref[0])

    @pl.when(n == 0)
    def _():
        ys_ref[...] = jnp.zeros(ys_ref.shape, F32)


def _moe_ffn(texp, nvalid, xs, wg, wu, wd, tm):
    n_tiles = xs.shape[0] // tm
    wspec = lambda s: pl.BlockSpec((1,) + s, lambda i, te, nv: (te[i], 0, 0))
    row = pl.BlockSpec((tm, D_MODEL), lambda i, te, nv: (i, 0))
    return pl.pallas_call(
        _moe_ffn_kernel,
        grid_spec=pltpu.PrefetchScalarGridSpec(
            num_scalar_prefetch=2,
            grid=(n_tiles,),
            in_specs=[row, wspec((D_MODEL, D_FF)), wspec((D_MODEL, D_FF)), wspec((D_FF, D_MODEL))],
            out_specs=row,
            scratch_shapes=[pltpu.VMEM((tm, D_FF), BF16)],
        ),
        out_shape=jax.ShapeDtypeStruct(xs.shape, F32),
        compiler_params=_params("arbitrary"),
        name="moe_ffn",
    )(texp, nvalid, xs, wg, wu, wd)


def _moe_combine_kernel(pos_ref, x_ref, g_ref, ys_hbm, o_ref, ybuf, sem, *, tt):
    _start_row_copies(pos_ref, tt, lambda k, j, src: pltpu.make_async_copy(
        ys_hbm.at[pl.ds(src, 1), :], ybuf.at[k, pl.ds(j, 1), :], sem))
    for k in range(2):
        pltpu.make_async_copy(ys_hbm.at[pl.ds(0, tt), :], ybuf.at[k], sem).wait()
    g = g_ref[...]
    o_ref[...] = x_ref[...] + (g[:, 0:1] * ybuf[0] + g[:, 1:2] * ybuf[1])


def _moe_combine(pos_tab, x, gates, ys):
    t = x.shape[0]
    nsteps, _, tt2 = pos_tab.shape
    tt = tt2 // 2
    kern = functools.partial(_moe_combine_kernel, tt=tt)
    row = lambda n: pl.BlockSpec((tt, n), lambda i: (i, 0))
    return pl.pallas_call(
        kern,
        grid=(nsteps,),
        in_specs=[pl.BlockSpec((1, 1, tt2), lambda i: (i, 0, 0), memory_space=pltpu.SMEM),
                  row(D_MODEL), row(2), pl.BlockSpec(memory_space=pl.ANY)],
        out_specs=row(D_MODEL),
        out_shape=jax.ShapeDtypeStruct((t, D_MODEL), F32),
        scratch_shapes=[pltpu.VMEM((2, tt, D_MODEL), F32), pltpu.SemaphoreType.DMA(())],
        compiler_params=_params("arbitrary"),
        name="moe_combine",
    )(pos_tab, x, gates, ys)


def _moe(x, gain, router, wg, wu, wd):
    t = x.shape[0]
    tm = TOKEN_TILE if 2 * t >= N_EXPERTS * 4 * TOKEN_TILE else TOKEN_TILE // 2
    comb, mask = _moe_route(x, gain, router)
    texp, nvalid, last_tile, pos_tab, gates, n_tiles = _moe_plan(mask, comb, tm)
    xs = _moe_dispatch(last_tile, pos_tab, x, gain, n_tiles * tm, tm)
    ys = _moe_ffn(texp, nvalid, xs, wg, wu, wd, tm)
    return _moe_combine(pos_tab, x, gates, ys)


def _pair_states(s):
    b = s.shape[0]
    s = s.reshape(b, RWKV_HEADS // 2, 2, RWKV_HEAD, RWKV_HEAD)
    z = jnp.zeros_like(s[:, :, 0])
    top = jnp.concatenate([s[:, :, 0], z], axis=-1)
    bot = jnp.concatenate([z, s[:, :, 1]], axis=-1)
    return jnp.concatenate([top, bot], axis=-2)


def _unpair_states(s):
    b = s.shape[0]
    s0 = s[:, :, :RWKV_HEAD, :RWKV_HEAD]
    s1 = s[:, :, RWKV_HEAD:, RWKV_HEAD:]
    return jnp.stack([s0, s1], axis=2).reshape(b, RWKV_HEADS, RWKV_HEAD, RWKV_HEAD)


def _trunk(x, ssm, conv, ck, cv, wkv, shift, w):
    b, l, _ = x.shape
    t = b * l
    ones_blk = w["ones_blk"]
    xf = x.reshape(t, D_MODEL)

    z, xbc, dt, q, k, v = _inproj(xf, w["norm_mix0"], w["wz"], w["wxbc"], w["wdt"], w["wq"], w["wk"], w["wv"],
                                  w["q_norm"], w["k_norm"], ones_blk)
    lp = l if l % SSD_CHUNK == 0 else 8
    seq = lambda a: a.reshape(b, l, a.shape[-1])
    padl = lambda a: a if lp == l else jnp.pad(a, ((0, 0), (0, lp - l), (0, 0)))
    y_ssd, ssm1, conv1 = _ssd(padl(seq(xbc)), padl(seq(dt)), padl(seq(z)), conv, ssm.reshape(b, SSD_WIDTH, SSD_STATE),
                              w["conv_w"], w["conv_b"], w["dt_bias"], w["a_log"], w["d_skip"], w["ssd_norm"], lv=l)
    y_ssd = y_ssd[:, :l].reshape(t, SSD_WIDTH)
    ssm1 = ssm1.reshape(b, SSD_HEADS, SSD_HEAD_DIM, SSD_STATE)
    if ck is None:
        o = _swa_prompt(seq(q), seq(k), seq(v), w["sinks"]).reshape(t, Q_DIM)
        k1 = seq(k)[:, -WINDOW:].reshape(b, WINDOW, ATT_KV_HEADS, ATT_HEAD_DIM)
        v1 = seq(v)[:, -WINDOW:].reshape(b, WINDOW, ATT_KV_HEADS, ATT_HEAD_DIM)
    else:
        wc = ck.shape[1]
        o, k1, v1 = _swa_cached(q, k, v, ck.reshape(b, wc, KV_DIM), cv.reshape(b, wc, KV_DIM), w["sinks"], l=l)
        k1 = k1.reshape(b, wc, ATT_KV_HEADS, ATT_HEAD_DIM)
        v1 = v1.reshape(b, wc, ATT_KV_HEADS, ATT_HEAD_DIM)
    x2 = _outproj_ffn(xf, y_ssd, o, w["wo_a"], w["wo_b"], w["norm_ffn0"], w["ffn_gate"], w["ffn_up"], w["ffn_down"])

    tm = min(TOKEN_TILE, t)
    if l >= tm:
        first = shift.reshape(b, 1, D_MODEL)
    else:
        first = jnp.repeat(shift, l, axis=0)
    r, lw, kx, vx, an, bb, gate, h1 = _rwkv_prep(
        x2, first, w["norm_mix1"], w["mu"], w["w0"], w["a0"], w["kk"], w["ka"], w["wr"], w["wkk"], w["wvv"],
        w["w1"], w["w2"], w["a1"], w["a2"], w["g1"], w["g2"], ones_blk, seq_len=l)
    shift1 = h1.reshape(b, D_MODEL) if l >= tm else h1.reshape(b, l, D_MODEL)[:, -1]
    lpr = -(-l // RWKV_CHUNK) * RWKV_CHUNK
    padr = lambda a: seq(a) if lpr == l else jnp.pad(seq(a), ((0, 0), (0, lpr - l), (0, 0)))
    y, s1 = _rwkv_scan(padr(r), padr(lw), padr(kx), padr(vx), padr(an), padr(bb), _pair_states(wkv))
    y = y[:, :l].reshape(t, D_MODEL)
    x3 = _rwkv_out(x2, y, r, kx, vx, gate, w["ln_w"], w["ln_b"], w["rk"], w["wo"], ones_blk)
    x4 = _moe(x3, w["norm_ffn1"], w["router"], w["moe_gate"], w["moe_up"], w["moe_down"])
    return (x4.reshape(b, l, D_MODEL), ssm1[None], conv1[None], k1[None], v1[None],
            _unpair_states(s1)[None], shift1[None])


def kernel(x_prompt, x_sample, state_ssm, state_conv, cache_swa_k, cache_swa_v, state_wkv, state_shift, norm_mix, norm_ffn, w_in, conv_w, conv_b, dt_bias, a_log, d_skip, ssd_norm, q_norm, k_norm, attn_sinks, w_out, ffn_gate, ffn_up, ffn_down, rwkv_mu, rwkv_w0, rwkv_w1, rwkv_w2, rwkv_a0, rwkv_a1, rwkv_a2, rwkv_g1, rwkv_g2, rwkv_kk, rwkv_ka, rwkv_rk, rwkv_wr, rwkv_wk, rwkv_wv, rwkv_wo, rwkv_ln_w, rwkv_ln_b, moe_router, moe_gate, moe_up, moe_down):
    bf = lambda a: a.astype(BF16)
    row = lambda a: a.reshape(1, -1).astype(F32)
    padlane = lambda a: jnp.pad(a, ((0, 0), (0, LANES - a.shape[1])))
    wi = w_in[0]
    c0 = SSD_WIDTH
    c1 = c0 + CONV_DIM
    c2 = c1 + SSD_HEADS
    c3 = c2 + Q_DIM
    c4 = c3 + KV_DIM
    w = dict(
        ones_blk=_block_ones(),
        norm_mix0=row(norm_mix[0]), norm_mix1=row(norm_mix[1]), norm_ffn0=row(norm_ffn[0]), norm_ffn1=row(norm_ffn[1]),
        wz=bf(wi[:, :c0]), wxbc=bf(wi[:, c0:c1]), wdt=bf(padlane(wi[:, c1:c2])), wq=bf(wi[:, c2:c3]),
        wk=bf(wi[:, c3:c4]), wv=bf(wi[:, c4:]),
        q_norm=row(jnp.tile(q_norm[0], ATT_HEADS)), k_norm=row(jnp.tile(k_norm[0], ATT_KV_HEADS)),
        conv_w=conv_w[0], conv_b=row(conv_b[0]), dt_bias=padlane(row(dt_bias[0])), a_log=padlane(row(a_log[0])),
        d_skip=row(jnp.repeat(d_skip[0], SSD_HEAD_DIM)), ssd_norm=row(ssd_norm[0]), sinks=attn_sinks[0].astype(F32),
        wo_a=bf(w_out[0, :SSD_WIDTH]), wo_b=bf(w_out[0, SSD_WIDTH:]),
        ffn_gate=bf(ffn_gate[0]), ffn_up=bf(ffn_up[0]), ffn_down=bf(ffn_down[0]),
        mu=rwkv_mu[0], w0=row(rwkv_w0[0]), a0=row(rwkv_a0[0]), kk=row(rwkv_kk[0]), ka=row(rwkv_ka[0]),
        wr=bf(rwkv_wr[0]), wkk=bf(rwkv_wk[0]), wvv=bf(rwkv_wv[0]), wo=bf(rwkv_wo[0]),
        w1=bf(rwkv_w1[0]), w2=bf(rwkv_w2[0]), a1=bf(rwkv_a1[0]), a2=bf(rwkv_a2[0]),
        g1=bf(rwkv_g1[0]), g2=bf(rwkv_g2[0]),
        ln_w=row(rwkv_ln_w[0]), ln_b=row(rwkv_ln_b[0]), rk=row(rwkv_rk[0]),
        router=padlane(moe_router[0]), moe_gate=bf(moe_gate[0]), moe_up=bf(moe_up[0]), moe_down=bf(moe_down[0]),
    )
    bp = x_prompt.shape[0]
    z_ssm = jnp.zeros((bp,) + state_ssm.shape[2:], F32)
    z_conv = jnp.zeros((bp,) + state_conv.shape[2:], F32)
    z_wkv = jnp.zeros((bp,) + state_wkv.shape[2:], F32)
    z_shift = jnp.zeros((bp,) + state_shift.shape[2:], F32)
    outs_p = _trunk(x_prompt, z_ssm, z_conv, None, None, z_wkv, z_shift, w)
    outs_s = _trunk(x_sample, state_ssm[0], state_conv[0], cache_swa_k[0], cache_swa_v[0], state_wkv[0],
                    state_shift[0], w)
    return (outs_p[0], outs_s[0]) + tuple(outs_p[1:]) + tuple(outs_s[1:])
```

```python
import functools

import jax
import jax.numpy as jnp
from jax import lax
from jax.experimental import pallas as pl
from jax.experimental.pallas import tpu as pltpu

F32 = jnp.float32
BF16 = jnp.bfloat16

D_MODEL = 1024
SSD_HEAD_DIM = 64
SSD_HEADS = 16
SSD_GROUPS = 2
SSD_STATE = 128
SSD_CONV = 4
SSD_CHUNK = 128
SSD_WIDTH = 1024
CONV_DIM = SSD_WIDTH + 2 * SSD_GROUPS * SSD_STATE
ATT_HEAD_DIM = 64
ATT_HEADS = 16
ATT_KV_HEADS = 4
ATT_REP = ATT_HEADS // ATT_KV_HEADS
WINDOW = 128
Q_DIM = ATT_HEADS * ATT_HEAD_DIM
KV_DIM = ATT_KV_HEADS * ATT_HEAD_DIM
RWKV_HEAD = 64
RWKV_HEADS = 16
RWKV_GN_EPS = 64e-5
RWKV_CHUNK = 64
D_FF = 2816
N_EXPERTS = 8
RMS_EPS = 1e-6
NEG_INF = -1e30

LANES = 128
MXU_DIM = 256
VMEM_LIMIT_BYTES = 56 * 1024 * 1024
TOKEN_TILE = 512
FF_CHUNK = MXU_DIM
HIGHEST = lax.Precision.HIGHEST


def _dot(a, b, precision=None):
    return jnp.dot(a, b, preferred_element_type=F32, precision=precision)


def _dot_nt(a, b, precision=None):
    return lax.dot_general(a, b, (((1,), (1,)), ((), ())), preferred_element_type=F32, precision=precision)


def _dot_tn(a, b, precision=None):
    return lax.dot_general(a, b, (((0,), (0,)), ((), ())), preferred_element_type=F32, precision=precision)


def _rms_rows(x, g):
    return x * lax.rsqrt(jnp.mean(x * x, axis=-1, keepdims=True) + RMS_EPS) * g


def _sigmoid(x):
    return 1.0 / (1.0 + jnp.exp(-x))


def _silu(x):
    return x * _sigmoid(x)


def _softplus(x):
    return jnp.maximum(x, 0.0) + jnp.log(1.0 + jnp.exp(-jnp.abs(x)))


def _const_spec(shape):
    zeros = (0,) * len(shape)
    return pl.BlockSpec(shape, lambda *_: zeros, pipeline_mode=pl.Buffered(1))


def _params(*semantics):
    return pltpu.CompilerParams(dimension_semantics=semantics, vmem_limit_bytes=VMEM_LIMIT_BYTES)


def _head_sums(t, ones_blk):
    return _dot(t.astype(BF16), ones_blk)


def _block_ones():
    r = jnp.arange(MXU_DIM) // ATT_HEAD_DIM
    return (r[:, None] == r[None, :]).astype(BF16)


def _inproj_kernel(x_ref, g_ref, wz_ref, wxbc_ref, wdt_ref, wq_ref, wk_ref, wv_ref, qn_ref, kn_ref, ones_ref,
                   z_ref, xbc_ref, dt_ref, q_ref, k_ref, v_ref):
    h = _rms_rows(x_ref[...], g_ref[...]).astype(BF16)
    ones_blk = ones_ref[...]

    def head_rms(t, gain):
        outs = []
        for c in range(t.shape[1] // MXU_DIM):
            tc = t[:, c * MXU_DIM:(c + 1) * MXU_DIM]
            ms = _head_sums(tc * tc, ones_blk) * (1.0 / ATT_HEAD_DIM)
            outs.append(tc * lax.rsqrt(ms + RMS_EPS))
        return jnp.concatenate(outs, axis=1) * gain

    z_ref[...] = _dot(h, wz_ref[...]).astype(z_ref.dtype)
    xbc_ref[...] = _dot(h, wxbc_ref[...])
    dt_ref[...] = _dot(h, wdt_ref[...])
    q_ref[...] = head_rms(_dot(h, wq_ref[...]), qn_ref[...]).astype(q_ref.dtype)
    k_ref[...] = head_rms(_dot(h, wk_ref[...]), kn_ref[...])
    v_ref[...] = _dot(h, wv_ref[...])


def _inproj(x, gain, wz, wxbc, wdt, wq, wk, wv, qn, kn, ones_blk):
    t = x.shape[0]
    tm = min(TOKEN_TILE, t)
    row = lambda n: pl.BlockSpec((tm, n), lambda i: (i, 0))
    return pl.pallas_call(
        _inproj_kernel,
        grid=(t // tm,),
        in_specs=[row(D_MODEL), _const_spec((1, D_MODEL)),
                  _const_spec(wz.shape), _const_spec(wxbc.shape), _const_spec(wdt.shape),
                  _const_spec(wq.shape), _const_spec(wk.shape), _const_spec(wv.shape),
                  _const_spec((1, Q_DIM)), _const_spec((1, KV_DIM)), _const_spec((MXU_DIM, MXU_DIM))],
        out_specs=[row(SSD_WIDTH), row(CONV_DIM), row(LANES), row(Q_DIM), row(KV_DIM), row(KV_DIM)],
        out_shape=[jax.ShapeDtypeStruct((t, SSD_WIDTH), BF16), jax.ShapeDtypeStruct((t, CONV_DIM), F32),
                   jax.ShapeDtypeStruct((t, LANES), F32), jax.ShapeDtypeStruct((t, Q_DIM), BF16),
                   jax.ShapeDtypeStruct((t, KV_DIM), F32), jax.ShapeDtypeStruct((t, KV_DIM), F32)],
        compiler_params=_params("arbitrary"),
        name="inproj",
    )(x, gain, wz, wxbc, wdt, wq, wk, wv, qn, kn, ones_blk)


def _ssd_kernel(xbc_ref, dt_ref, z_ref, conv0_ref, ssm0_ref, cw_ref, cb_ref, dtb_ref, alog_ref, dsk_ref, nrm_ref,
                y_ref, ssm1_ref, conv1_ref, xpad_ref, dtpad_ref, s_ref, *, q, lin, lv, nchunks):
    c = pl.program_id(1)
    n_pairs = SSD_HEADS // 2
    hist = SSD_CONV - 1
    base = 8 - hist

    @pl.when(c == 0)
    def _():
        xpad_ref[...] = jnp.zeros(xpad_ref.shape, F32)
        dtpad_ref[...] = jnp.zeros(dtpad_ref.shape, F32)
        xpad_ref[base:8, :] = conv0_ref[0]
        for jb in range(n_pairs):
            s_ref[:, jb * LANES:(jb + 1) * LANES] = ssm0_ref[0, jb * LANES:(jb + 1) * LANES, :].T

    xpad_ref[8:8 + lin, :] = xbc_ref[0]
    dtpad_ref[0:lin, :] = dt_ref[0]

    conv = cb_ref[...]
    for j in range(SSD_CONV):
        conv = conv + xpad_ref[base + j:base + j + q, :] * cw_ref[j:j + 1, :]
    tail = xpad_ref[base + lv:8 + lv, :]
    conv1_ref[0] = tail
    xpad_ref[base:8, :] = tail

    xc = _silu(conv)
    xs = xc[:, :SSD_WIDTH]
    bm = xc[:, SSD_WIDTH:SSD_WIDTH + SSD_GROUPS * SSD_STATE].astype(BF16)
    cm = xc[:, SSD_WIDTH + SSD_GROUPS * SSD_STATE:].astype(BF16)

    row_i = lax.broadcasted_iota(jnp.int32, (q, LANES), 0)
    col_i = lax.broadcasted_iota(jnp.int32, (q, LANES), 1)
    first_half = col_i < SSD_HEAD_DIM
    dt = jnp.where(row_i < lv, _softplus(dtpad_ref[...] + dtb_ref[...]), 0.0)
    a = dt * (-jnp.exp(alog_ref[...]))
    r_q = lax.broadcasted_iota(jnp.int32, (q, q), 0)
    c_q = lax.broadcasted_iota(jnp.int32, (q, q), 1)
    causal = r_q >= c_q
    a_cs = _dot(causal.astype(F32), a, precision=HIGHEST)
    a_cs_t = a_cs.T
    ea = jnp.exp(a_cs)
    dte = jnp.exp(a_cs[q - 1:q, :] - a_cs)

    def colb(arr, h):
        return jnp.broadcast_to(arr[:, h:h + 1], (q, LANES))

    def pairb(arr, h0):
        return jnp.where(first_half, colb(arr, h0), colb(arr, h0 + 1))

    cbs = []
    for g in range(SSD_GROUPS):
        sl = slice(g * SSD_STATE, (g + 1) * SSD_STATE)
        cbs.append(_dot_nt(cm[:, sl], bm[:, sl]))

    ys = []
    for j in range(n_pairs):
        g = (2 * j) // (SSD_HEADS // SSD_GROUPS)
        gsl = slice(g * SSD_STATE, (g + 1) * SSD_STATE)
        psl = slice(j * LANES, (j + 1) * LANES)
        xs_p = xs[:, psl]
        xdt = xs_p * pairb(dt, 2 * j)
        xdt_b = xdt.astype(BF16)
        yd = []
        for hh in (2 * j, 2 * j + 1):
            diff = colb(a_cs, hh) - a_cs_t[hh:hh + 1, :]
            lmat = jnp.exp(jnp.where(causal, diff, NEG_INF))
            yd.append(_dot((cbs[g] * lmat).astype(BF16), xdt_b))
        y_diag = jnp.where(first_half, yd[0], yd[1])
        ea_p = pairb(ea, 2 * j)
        s_old = s_ref[:, psl]
        y_off = _dot(cm[:, gsl], s_old.astype(BF16)) * ea_p
        s_ref[:, psl] = s_old * ea_p[q - 1:q, :] + _dot_tn(bm[:, gsl], (xdt * pairb(dte, 2 * j)).astype(BF16))
        ys.append(y_diag + y_off + xs_p * dsk_ref[:, psl])
    y = jnp.concatenate(ys, axis=1)[0:lin]

    y = y * _silu(z_ref[0].astype(F32))
    gw = SSD_WIDTH // SSD_GROUPS
    outs = []
    for g in range(SSD_GROUPS):
        yg = y[:, g * gw:(g + 1) * gw]
        outs.append(yg * lax.rsqrt(jnp.mean(yg * yg, axis=-1, keepdims=True) + RMS_EPS))
    y_ref[0] = (jnp.concatenate(outs, axis=1) * nrm_ref[...]).astype(y_ref.dtype)

    @pl.when(c == nchunks - 1)
    def _():
        for jb in range(n_pairs):
            ssm1_ref[0, jb * LANES:(jb + 1) * LANES, :] = s_ref[:, jb * LANES:(jb + 1) * LANES].T


def _ssd(xbc, dt, z, conv0, ssm0, cw, cb, dtb, alog, dsk, nrm, *, lv):
    b, lp, _ = xbc.shape
    q = SSD_CHUNK
    lin = min(q, lp)
    nchunks = lp // lin
    lv_chunk = min(lv, lin)
    kern = functools.partial(_ssd_kernel, q=q, lin=lin, lv=lv_chunk, nchunks=nchunks)
    seq = lambda n: pl.BlockSpec((1, lin, n), lambda i, c: (i, c, 0))
    per_b = lambda s: pl.BlockSpec((1,) + s, lambda i, c: (i, 0, 0))
    return pl.pallas_call(
        kern,
        grid=(b, nchunks),
        in_specs=[seq(CONV_DIM), seq(LANES), seq(SSD_WIDTH),
                  per_b((SSD_CONV - 1, CONV_DIM)), per_b((SSD_WIDTH, SSD_STATE)),
                  _const_spec((SSD_CONV, CONV_DIM)), _const_spec((1, CONV_DIM)), _const_spec((1, LANES)),
                  _const_spec((1, LANES)), _const_spec((1, SSD_WIDTH)), _const_spec((1, SSD_WIDTH))],
        out_specs=[seq(SSD_WIDTH), per_b((SSD_WIDTH, SSD_STATE)), per_b((SSD_CONV - 1, CONV_DIM))],
        out_shape=[jax.ShapeDtypeStruct((b, lp, SSD_WIDTH), BF16),
                   jax.ShapeDtypeStruct((b, SSD_WIDTH, SSD_STATE), F32),
                   jax.ShapeDtypeStruct((b, SSD_CONV - 1, CONV_DIM), F32)],
        scratch_shapes=[pltpu.VMEM((q + 8, CONV_DIM), F32), pltpu.VMEM((q, LANES), F32),
                        pltpu.VMEM((SSD_STATE, SSD_WIDTH), F32)],
        compiler_params=_params("arbitrary", "arbitrary"),
        name="ssd",
    )(xbc, dt, z, conv0, ssm0, cw, cb, dtb, alog, dsk, nrm)


def _sink_softmax(s, allowed, sink_col):
    s = jnp.where(allowed, s, NEG_INF)
    m = jnp.maximum(jnp.max(s, axis=-1, keepdims=True), sink_col)
    return jnp.exp(s - m).astype(BF16), jnp.exp(sink_col - m)


def _gqa_blocks(blocks, sink_ref, allowed, nq, group):
    d = ATT_HEAD_DIM
    sinks = [jnp.concatenate([jnp.full((nq, 1), sink_ref[ATT_REP * j + r], F32) for r in range(ATT_REP)], axis=0)
             for j in range(ATT_KV_HEADS)]
    problems = [(bi, j) for bi in range(len(blocks)) for j in range(ATT_KV_HEADS)]
    ones_k = jnp.ones((blocks[0][1].shape[0], d), BF16)
    outs = {}
    for g0 in range(0, len(problems), group):
        grp = problems[g0:g0 + group]
        scores = []
        for bi, j in grp:
            qb, kk, _ = blocks[bi]
            qj = jnp.concatenate(
                [qb[:, (ATT_REP * j + r) * d:(ATT_REP * j + r + 1) * d] for r in range(ATT_REP)], axis=0)
            scores.append(_dot_nt(qj * (d ** -0.5), kk[:, j * d:(j + 1) * d]))
        probs = [_sink_softmax(s, allowed, sinks[j]) for s, (_, j) in zip(scores, grp)]
        for (p, sink_term), (bi, j) in zip(probs, grp):
            denom = _dot(p, ones_k) + sink_term
            outs[bi, j] = _dot(p, blocks[bi][2][:, j * d:(j + 1) * d]) / denom
    return [jnp.concatenate([outs[bi, j][r * nq:(r + 1) * nq] for j in range(ATT_KV_HEADS) for r in range(ATT_REP)],
                            axis=1) for bi in range(len(blocks))]


def _swa_prompt_kernel(sink_ref, q_ref, kp_ref, kc_ref, vp_ref, vc_ref, o_ref):
    i = pl.program_id(1)
    w = WINDOW
    kk = jnp.concatenate([kp_ref[0], kc_ref[0]], axis=0).astype(BF16)
    vv = jnp.concatenate([vp_ref[0], vc_ref[0]], axis=0).astype(BF16)
    rows = lax.broadcasted_iota(jnp.int32, (ATT_REP * w, 2 * w), 0) % w
    cols = lax.broadcasted_iota(jnp.int32, (ATT_REP * w, 2 * w), 1)
    rel = rows + w - cols
    allowed = (rel >= 0) & (rel <= w) & ((cols >= w) | (i > 0))
    o_ref[0] = _gqa_blocks([(q_ref[0], kk, vv)], sink_ref, allowed, w, group=ATT_KV_HEADS)[0].astype(o_ref.dtype)


def _swa_prompt(q, k, v, sinks):
    b, l, _ = q.shape
    nb = l // WINDOW
    cur = lambda n: pl.BlockSpec((1, WINDOW, n), lambda i, c: (i, c, 0))
    prev = lambda n: pl.BlockSpec((1, WINDOW, n), lambda i, c: (i, jnp.maximum(c - 1, 0), 0))
    return pl.pallas_call(
        _swa_prompt_kernel,
        grid=(b, nb),
        in_specs=[pl.BlockSpec(memory_space=pltpu.SMEM),
                  cur(Q_DIM), prev(KV_DIM), cur(KV_DIM), prev(KV_DIM), cur(KV_DIM)],
        out_specs=cur(Q_DIM),
        out_shape=jax.ShapeDtypeStruct((b, l, Q_DIM), BF16),
        compiler_params=_params("arbitrary", "arbitrary"),
        name="swa_prompt",
    )(sinks, q, k, k, v, v)


def _swa_cached_kernel(sink_ref, q_ref, k_ref, v_ref, kbuf_ref, vbuf_ref, o_ref, k1_ref, v1_ref, *, nbatch, l):
    wc = kbuf_ref.shape[1]
    pad = 8 - l
    nk = wc + 8
    rows = lax.broadcasted_iota(jnp.int32, (ATT_REP * l, nk), 0) % l
    cols = lax.broadcasted_iota(jnp.int32, (ATT_REP * l, nk), 1)
    rel = rows + wc - cols
    allowed = (rel >= 0) & (rel <= WINDOW)
    zpad = jnp.zeros((pad, KV_DIM), F32)
    qall = q_ref[...].astype(F32)
    blocks = []
    for bi in range(nbatch):
        kn = k_ref[bi * l:(bi + 1) * l, :]
        vn = v_ref[bi * l:(bi + 1) * l, :]
        kc = jnp.concatenate([kbuf_ref[bi], kn, zpad], axis=0).astype(BF16)
        vc = jnp.concatenate([vbuf_ref[bi], vn, zpad], axis=0).astype(BF16)
        blocks.append((qall[bi * l:(bi + 1) * l, :].astype(BF16), kc, vc))
        k1_ref[bi, 0:wc - l, :] = kbuf_ref[bi, l:wc, :]
        k1_ref[bi, wc - l:wc, :] = kn
        v1_ref[bi, 0:wc - l, :] = vbuf_ref[bi, l:wc, :]
        v1_ref[bi, wc - l:wc, :] = vn
    o_ref[...] = jnp.concatenate(_gqa_blocks(blocks, sink_ref, allowed, l, group=ATT_KV_HEADS * nbatch), axis=0).astype(o_ref.dtype)


def _swa_cached(q, k, v, kbuf, vbuf, sinks, *, l):
    b, wc, _ = kbuf.shape
    nbatch = 8
    kern = functools.partial(_swa_cached_kernel, nbatch=nbatch, l=l)
    rows = lambda n: pl.BlockSpec((nbatch * l, n), lambda i: (i, 0))
    cache = pl.BlockSpec((nbatch, wc, KV_DIM), lambda i: (i, 0, 0))
    return pl.pallas_call(
        kern,
        grid=(b // nbatch,),
        in_specs=[pl.BlockSpec(memory_space=pltpu.SMEM), rows(Q_DIM), rows(KV_DIM), rows(KV_DIM), cache, cache],
        out_specs=[rows(Q_DIM), cache, cache],
        out_shape=[jax.ShapeDtypeStruct((b * l, Q_DIM), BF16),
                   jax.ShapeDtypeStruct((b, wc, KV_DIM), F32), jax.ShapeDtypeStruct((b, wc, KV_DIM), F32)],
        compiler_params=_params("arbitrary"),
        name="swa_cached",
    )(sinks, q, k, v, kbuf, vbuf)


def _swiglu_act(h, wg_ref, wu_ref, act_ref):
    for c in range(D_FF // FF_CHUNK):
        sl = slice(c * FF_CHUNK, (c + 1) * FF_CHUNK)
        act_ref[:, sl] = (_silu(_dot(h, wg_ref[:, sl])) * _dot(h, wu_ref[:, sl])).astype(BF16)


def _outproj_ffn_kernel(x_ref, ya_ref, yb_ref, woa_ref, wob_ref, g_ref, wg_ref, wu_ref, wd_ref, o_ref, act_ref):
    x1 = x_ref[...] + _dot(ya_ref[...], woa_ref[...]) + _dot(yb_ref[...], wob_ref[...])
    h = _rms_rows(x1, g_ref[...]).astype(BF16)
    _swiglu_act(h, wg_ref, wu_ref, act_ref)
    o_ref[...] = x1 + _dot(act_ref[...], wd_ref[...])


def _outproj_ffn(x, ya, yb, woa, wob, gain, wg, wu, wd):
    t = x.shape[0]
    tm = min(TOKEN_TILE, t)
    row = lambda n: pl.BlockSpec((tm, n), lambda i: (i, 0))
    return pl.pallas_call(
        _outproj_ffn_kernel,
        grid=(t // tm,),
        in_specs=[row(D_MODEL), row(SSD_WIDTH), row(Q_DIM), _const_spec(woa.shape), _const_spec(wob.shape),
                  _const_spec((1, D_MODEL)), _const_spec(wg.shape), _const_spec(wu.shape), _const_spec(wd.shape)],
        out_specs=row(D_MODEL),
        out_shape=jax.ShapeDtypeStruct((t, D_MODEL), F32),
        scratch_shapes=[pltpu.VMEM((tm, D_FF), BF16)],
        compiler_params=_params("arbitrary"),
        name="outproj_ffn",
    )(x, ya, yb, woa, wob, gain, wg, wu, wd)


def _rwkv_prep_kernel(x_ref, first_ref, g_ref, mu_ref, w0_ref, a0_ref, kkc_ref, ka_ref,
                      wr_ref, wk_ref, wv_ref, w1_ref, w2_ref, a1_ref, a2_ref, g1_ref, g2_ref, ones_ref,
                      r_ref, lw_ref, k_ref, v_ref, an_ref, b_ref, gate_ref, h_ref, hs_ref, *, tm, seq_len):
    i = pl.program_id(0)
    h = _rms_rows(x_ref[...], g_ref[...])
    tiles_per_seq = max(seq_len // tm, 1)

    if seq_len >= tm:
        h_ref[0] = h[tm - 1:tm, :]

        @pl.when(i % tiles_per_seq == 0)
        def _():
            hs_ref[7:8, :] = first_ref[0]
    else:
        h_ref[...] = h
        hs_ref[0:8, :] = jnp.zeros((8, D_MODEL), F32)
    hs_ref[8:8 + tm, :] = h
    prev = hs_ref[7:7 + tm, :]
    if seq_len < tm:
        rows = lax.broadcasted_iota(jnp.int32, (tm, D_MODEL), 0)
        prev = jnp.where(rows % seq_len == 0, first_ref[...], prev)
    else:
        hs_ref[7:8, :] = h[tm - 1:tm, :]
    xx = prev - h
    mix = lambda n: (h + xx * mu_ref[n:n + 1, :]).astype(BF16)
    r = _dot(mix(0), wr_ref[...])
    k = _dot(mix(2), wk_ref[...])
    v = _dot(mix(3), wv_ref[...])
    w_lora = _dot(jnp.tanh(_dot(mix(1), w1_ref[...])).astype(BF16), w2_ref[...])
    a_lora = _dot(_dot(mix(4), a1_ref[...]).astype(BF16), a2_ref[...])
    gate = _dot(_sigmoid(_dot(mix(5), g1_ref[...])).astype(BF16), g2_ref[...])
    w_log = -_softplus(-(w0_ref[...] + w_lora)) - 0.5
    a_sig = _sigmoid(a0_ref[...] + a_lora)
    kk = k * kkc_ref[...]
    ones_blk = ones_ref[...]
    parts = []
    for c in range(D_MODEL // MXU_DIM):
        kc = kk[:, c * MXU_DIM:(c + 1) * MXU_DIM]
        nrm = jnp.maximum(jnp.sqrt(_head_sums(kc * kc, ones_blk)), 1e-12)
        parts.append(kc / nrm)
    kk = jnp.concatenate(parts, axis=1)
    r_ref[...] = r.astype(r_ref.dtype)
    lw_ref[...] = -jnp.exp(w_log)
    k_ref[...] = (k * (1.0 + (a_sig - 1.0) * ka_ref[...])).astype(k_ref.dtype)
    v_ref[...] = v.astype(v_ref.dtype)
    an_ref[...] = (-kk).astype(an_ref.dtype)
    b_ref[...] = (kk * a_sig).astype(b_ref.dtype)
    gate_ref[...] = gate.astype(gate_ref.dtype)


def _rwkv_prep(x, first, gain, mu, w0, a0, kkc, ka, wr, wk, wv, w1, w2, a1, a2, g1, g2, ones_blk, *, seq_len):
    t = x.shape[0]
    tm = min(TOKEN_TILE, t)
    kern = functools.partial(_rwkv_prep_kernel, tm=tm, seq_len=seq_len)
    row = lambda n: pl.BlockSpec((tm, n), lambda i: (i, 0))
    if seq_len >= tm:
        tiles_per_seq = seq_len // tm
        first_spec = pl.BlockSpec((1, 1, D_MODEL), lambda i: (i // tiles_per_seq, 0, 0))
        last_spec = first_spec
        last_shape = jax.ShapeDtypeStruct((t // seq_len, 1, D_MODEL), F32)
    else:
        first_spec = row(D_MODEL)
        last_spec = row(D_MODEL)
        last_shape = jax.ShapeDtypeStruct((t, D_MODEL), F32)
    vec = _const_spec((1, D_MODEL))
    wide = lambda dt: jax.ShapeDtypeStruct((t, D_MODEL), dt)
    outs = [wide(BF16), wide(F32), wide(BF16), wide(BF16), wide(BF16), wide(BF16), wide(BF16), last_shape]
    return pl.pallas_call(
        kern,
        grid=(t // tm,),
        in_specs=[row(D_MODEL), first_spec, vec, _const_spec((6, D_MODEL)), vec, vec, vec, vec,
                  _const_spec(wr.shape), _const_spec(wk.shape), _const_spec(wv.shape),
                  _const_spec(w1.shape), _const_spec(w2.shape), _const_spec(a1.shape), _const_spec(a2.shape),
                  _const_spec(g1.shape), _const_spec(g2.shape), _const_spec((MXU_DIM, MXU_DIM))],
        out_specs=[row(D_MODEL)] * 7 + [last_spec],
        out_shape=outs,
        scratch_shapes=[pltpu.VMEM((tm + 8, D_MODEL), F32)],
        compiler_params=_params("arbitrary"),
        name="rwkv_prep",
    )(x, first, gain, mu, w0, a0, kkc, ka, wr, wk, wv, w1, w2, a1, a2, g1, g2, ones_blk)


def _rwkv_scan_kernel(r_ref, lw_ref, k_ref, v_ref, a_ref, b_ref, s0_ref, y_ref, s1_ref, s_ref, *, nsub, nsteps):
    step = pl.program_id(1)
    c = RWKV_CHUNK
    n_pairs = RWKV_HEADS // 2
    two_c = 2 * c

    @pl.when(step == 0)
    def _():
        s_ref[...] = s0_ref[0]

    lane = lax.broadcasted_iota(jnp.int32, (c, LANES), 1)
    m_a = lane < RWKV_HEAD
    r_i = lax.broadcasted_iota(jnp.int32, (two_c, two_c), 0)
    c_i = lax.broadcasted_iota(jnp.int32, (two_c, two_c), 1)
    strict = r_i > c_i
    blk = (r_i < c) == (c_i < c)
    r_i2 = lax.broadcasted_iota(jnp.int32, (two_c, 2 * two_c), 0)
    c_i2 = lax.broadcasted_iota(jnp.int32, (two_c, 2 * two_c), 1) % two_c
    incl2 = r_i2 >= c_i2
    t_r = lax.broadcasted_iota(jnp.int32, (c, c), 0)
    t_c = lax.broadcasted_iota(jnp.int32, (c, c), 1)
    tri = (t_r >= t_c).astype(F32)

    def stack(x):
        return jnp.concatenate([jnp.where(m_a, x, 0.0), jnp.where(m_a, 0.0, x)], axis=0)

    def sub_chunk(ci, carry):
        rows = pl.ds(pl.multiple_of(ci * c, c), c)
        lw = lw_ref[0, rows, :]
        cl = _dot(tri, lw, precision=HIGHEST)
        cl_end = cl[c - 1:c, :]
        e_neg = jnp.exp(-cl)
        e_end = jnp.exp(cl_end - cl)
        w_end = jnp.exp(cl_end)
        aa = a_ref[0, rows, :].astype(F32) * jnp.exp(cl - lw)
        rr = r_ref[0, rows, :].astype(F32) * jnp.exp(cl)
        bb = b_ref[0, rows, :].astype(F32)
        kk = k_ref[0, rows, :].astype(F32)
        vv = v_ref[0, rows, :].astype(F32)
        pairs = range(n_pairs)
        sl = [slice(p * LANES, (p + 1) * LANES) for p in pairs]
        lhs = [jnp.concatenate([stack(aa[:, sl[p]]), stack(rr[:, sl[p]])], axis=0).astype(BF16) for p in pairs]
        rhs = [jnp.concatenate([stack((bb * e_neg)[:, sl[p]]), stack((kk * e_neg)[:, sl[p]])], axis=0).astype(BF16)
               for p in pairs]
        bk = [jnp.concatenate([stack((bb * e_end)[:, sl[p]]), stack((kk * e_end)[:, sl[p]])], axis=0).astype(BF16)
              for p in pairs]
        v_s = [stack(vv[:, sl[p]]) for p in pairs]
        s_old = [s_ref[p] for p in pairs]
        gmat = [_dot_nt(lhs[p], rhs[p]) for p in pairs]
        ph = [_dot_nt(lhs[p], s_old[p].astype(BF16)) for p in pairs]
        npow = [jnp.where(strict, gmat[p][:two_c, :two_c], 0.0).astype(BF16) for p in pairs]
        a_ak = [jnp.where(strict, gmat[p][:two_c, two_c:], 0.0).astype(BF16) for p in pairs]
        a_r = [jnp.where(incl2, gmat[p][two_c:, :], 0.0).astype(BF16) for p in pairs]
        u = [ph[p][:two_c] + _dot(a_ak[p], v_s[p].astype(BF16)) for p in pairs]
        for lvl in range(5):
            prod = [_dot(npow[p], jnp.concatenate([u[p].astype(BF16), npow[p]], axis=1)) for p in pairs]
            u = [u[p] + prod[p][:, :LANES] for p in pairs]
            npow = [prod[p][:, LANES:].astype(BF16) for p in pairs]
        u = [u[p] + _dot(npow[p], u[p].astype(BF16)) for p in pairs]
        uv = [jnp.concatenate([u[p], v_s[p]], axis=0).astype(BF16) for p in pairs]
        y_s = [ph[p][two_c:] + _dot(a_r[p], uv[p]) for p in pairs]
        y_ref[0, rows, :] = jnp.concatenate([y_s[p][:c] + y_s[p][c:] for p in pairs], axis=1).astype(y_ref.dtype)
        for p in pairs:
            s_new = s_old[p] * w_end[:, sl[p]] + _dot_tn(uv[p], bk[p])
            s_ref[p] = jnp.where(blk, s_new, 0.0)
        return carry

    lax.fori_loop(0, nsub, sub_chunk, 0)

    @pl.when(step == nsteps - 1)
    def _():
        s1_ref[0] = s_ref[...]


def _rwkv_scan(r, lw, k, v, an, b, s0):
    bsz, lp, _ = r.shape
    rows = min(lp, 512)
    nsteps = lp // rows
    nsub = rows // RWKV_CHUNK
    kern = functools.partial(_rwkv_scan_kernel, nsub=nsub, nsteps=nsteps)
    seq = pl.BlockSpec((1, rows, D_MODEL), lambda i, s: (i, s, 0))
    st = pl.BlockSpec((1, RWKV_HEADS // 2, LANES, LANES), lambda i, s: (i, 0, 0, 0))
    return pl.pallas_call(
        kern,
        grid=(bsz, nsteps),
        in_specs=[seq] * 6 + [st],
        out_specs=[seq, st],
        out_shape=[jax.ShapeDtypeStruct((bsz, lp, D_MODEL), BF16),
                   jax.ShapeDtypeStruct((bsz, RWKV_HEADS // 2, LANES, LANES), F32)],
        scratch_shapes=[pltpu.VMEM((RWKV_HEADS // 2, LANES, LANES), F32)],
        compiler_params=_params("arbitrary", "arbitrary"),
        name="rwkv_scan",
    )(r, lw, k, v, an, b, s0)


def _route_top2(h, router):
    logits = _dot(h, router, precision=HIGHEST)
    lane = lax.broadcasted_iota(jnp.int32, logits.shape, 1)
    logits = jnp.where(lane < N_EXPERTS, logits, -jnp.inf)
    m1 = jnp.max(logits, axis=-1, keepdims=True)
    i1 = jnp.min(jnp.where(logits == m1, lane, LANES), axis=-1, keepdims=True)
    rest = jnp.where(lane == i1, -jnp.inf, logits)
    m2 = jnp.max(rest, axis=-1, keepdims=True)
    i2 = jnp.min(jnp.where(rest == m2, lane, LANES), axis=-1, keepdims=True)
    e2 = jnp.exp(m2 - m1)
    g1 = 1.0 / (1.0 + e2)
    g2 = e2 / (1.0 + e2)
    comb = jnp.where(lane == i1, g1, 0.0) + jnp.where(lane == i2, g2, 0.0)
    mask = jnp.where((lane == i1) | (lane == i2), 1.0, 0.0)
    return comb, mask


def _rwkv_out_kernel(x_ref, y_ref, r_ref, k_ref, v_ref, gate_ref, lnw_ref, lnb_ref, rk_ref, wo_ref, ones_ref,
                     gffn_ref, router_ref, o_ref, comb_ref, mask_ref):
    ones_blk = ones_ref[...]
    inv = 1.0 / RWKV_HEAD
    parts = []
    for c in range(D_MODEL // MXU_DIM):
        sl = slice(c * MXU_DIM, (c + 1) * MXU_DIM)
        yh = y_ref[:, sl]
        y = yh.astype(F32)
        yc = y - _dot(yh, ones_blk) * inv
        var = _head_sums(yc * yc, ones_blk) * inv
        yn = yc * lax.rsqrt(var + RWKV_GN_EPS) * lnw_ref[:, sl] + lnb_ref[:, sl]
        rk = r_ref[:, sl].astype(F32) * k_ref[:, sl].astype(F32) * rk_ref[:, sl]
        rkh = rk.astype(BF16)
        bonus = _dot(rkh, ones_blk) + _head_sums(rk - rkh.astype(F32), ones_blk)
        parts.append(yn + bonus * v_ref[:, sl].astype(F32))
    out = jnp.concatenate(parts, axis=1) * gate_ref[...].astype(F32)
    x3 = x_ref[...] + _dot(out.astype(BF16), wo_ref[...])
    o_ref[...] = x3
    comb_ref[...], mask_ref[...] = _route_top2(_rms_rows(x3, gffn_ref[...]), router_ref[...])


def _rwkv_out(x, y, r, k, v, gate, lnw, lnb, rk, wo, ones_blk, gain_ffn, router):
    t = x.shape[0]
    tm = min(TOKEN_TILE, t)
    row = lambda n: pl.BlockSpec((tm, n), lambda i: (i, 0))
    vec = _const_spec((1, D_MODEL))
    return pl.pallas_call(
        _rwkv_out_kernel,
        grid=(t // tm,),
        in_specs=[row(D_MODEL)] * 6 + [vec, vec, vec, _const_spec(wo.shape), _const_spec((MXU_DIM, MXU_DIM)),
                                       vec, _const_spec((D_MODEL, LANES))],
        out_specs=[row(D_MODEL), row(LANES), row(LANES)],
        out_shape=[jax.ShapeDtypeStruct((t, D_MODEL), F32), jax.ShapeDtypeStruct((t, LANES), F32),
                   jax.ShapeDtypeStruct((t, LANES), F32)],
        compiler_params=_params("arbitrary"),
        name="rwkv_out",
    )(x, y, r, k, v, gate, lnw, lnb, rk, wo, ones_blk, gain_ffn, router)


def _moe_plan(mask, comb, tm):
    m = mask[:, :N_EXPERTS].astype(jnp.int32)
    t = m.shape[0]
    tt = min(TOKEN_TILE, t)
    n_tiles = (2 * t) // tm + N_EXPERTS
    rank = jnp.cumsum(m, axis=0)
    counts = rank[-1]
    tiles_e = (counts + tm - 1) // tm
    tile_end = jnp.cumsum(tiles_e)
    tile_start = tile_end - tiles_e
    pos = (tile_start * tm)[None, :] + rank - 1
    slot = jnp.cumsum(m, axis=1) - 1
    c8 = comb[:, :N_EXPERTS]
    pick = [(slot == k) & (m > 0) for k in range(2)]
    pos_k = jnp.stack([jnp.sum(jnp.where(pk, pos, 0), axis=1) for pk in pick], axis=0)
    gates = jnp.stack([jnp.sum(jnp.where(pk, c8, 0.0), axis=1) for pk in pick], axis=1)
    pos_tab = pos_k.reshape(2, t // tt, tt).transpose(1, 0, 2).reshape(t // tt, 1, 2 * tt)
    tile_ids = jnp.arange(n_tiles, dtype=jnp.int32)
    texp = jnp.sum((tile_ids[:, None] >= tile_end[None, :]).astype(jnp.int32), axis=1)
    texp_c = jnp.minimum(texp, N_EXPERTS - 1)
    nvalid = jnp.clip(counts[texp_c] - (tile_ids - tile_start[texp_c]) * tm, 0, tm)
    nvalid = jnp.where(texp < N_EXPERTS, nvalid, 0).astype(jnp.int32)
    last_tile = jnp.concatenate([jnp.where(tiles_e > 0, tile_end - 1, -1), tile_end[-1:]]).astype(jnp.int32)
    return texp_c.astype(jnp.int32), nvalid, last_tile, pos_tab.astype(jnp.int32), gates, n_tiles


def _start_row_copies(pos_ref, tt, make):
    def body(j, c):
        for k in range(2):
            make(k, j, pos_ref[0, 0, k * tt + j]).start(priority=k)
        return c
    lax.fori_loop(0, tt, body, 0, unroll=8)


def _moe_dispatch_kernel(last_ref, pos_ref, x_ref, g_ref, xs_hbm, hbuf, zbuf, sem, zsem, *, tt, tm, n_tiles):
    def zero_tile(tile):
        return pltpu.make_async_copy(zbuf, xs_hbm.at[pl.ds(pl.multiple_of(tile * tm, tm), tm), :], zsem)

    @pl.when(pl.program_id(0) == 0)
    def _():
        zbuf[...] = jnp.zeros(zbuf.shape, F32)
        for e in range(N_EXPERTS):
            @pl.when(last_ref[e] >= 0)
            def _():
                zero_tile(last_ref[e]).start()
        for e in range(N_EXPERTS):
            @pl.when(last_ref[e] >= 0)
            def _():
                zero_tile(last_ref[e]).wait()

        def clear_unused(tile, c):
            zero_tile(tile).start()
            zero_tile(tile).wait()
            return c
        lax.fori_loop(last_ref[N_EXPERTS], n_tiles, clear_unused, 0)

    hbuf[...] = _rms_rows(x_ref[...], g_ref[...])
    _start_row_copies(pos_ref, tt, lambda k, j, dst: pltpu.make_async_copy(
        hbuf.at[pl.ds(j, 1), :], xs_hbm.at[pl.ds(dst, 1), :], sem))
    for _ in range(2):
        pltpu.make_async_copy(hbuf, xs_hbm.at[pl.ds(0, tt), :], sem).wait()


def _moe_dispatch(last_tile, pos_tab, x, gain, n_rows, tm):
    nsteps, _, tt2 = pos_tab.shape
    tt = tt2 // 2
    kern = functools.partial(_moe_dispatch_kernel, tt=tt, tm=tm, n_tiles=n_rows // tm)
    return pl.pallas_call(
        kern,
        grid_spec=pltpu.PrefetchScalarGridSpec(
            num_scalar_prefetch=1,
            grid=(nsteps,),
            in_specs=[pl.BlockSpec((1, 1, tt2), lambda i, lt: (i, 0, 0), memory_space=pltpu.SMEM),
                      pl.BlockSpec((tt, D_MODEL), lambda i, lt: (i, 0)),
                      pl.BlockSpec((1, D_MODEL), lambda i, lt: (0, 0))],
            out_specs=pl.BlockSpec(memory_space=pl.ANY),
            scratch_shapes=[pltpu.VMEM((tt, D_MODEL), F32), pltpu.VMEM((tm, D_MODEL), F32),
                            pltpu.SemaphoreType.DMA(()), pltpu.SemaphoreType.DMA(())],
        ),
        out_shape=jax.ShapeDtypeStruct((n_rows, D_MODEL), F32),
        compiler_params=_params("arbitrary"),
        name="moe_dispatch",
    )(last_tile, pos_tab, x, gain)


def _moe_ffn_kernel(texp_ref, nvalid_ref, xs_ref, wg_ref, wu_ref, wd_ref, ys_ref, act_ref):
    n = nvalid_ref[pl.program_id(0)]

    @pl.when(n > 0)
    def _():
        _swiglu_act(xs_ref[...].astype(BF16), wg_ref.at[0], wu_ref.at[0], act_ref)
        ys_ref[...] = _dot(act_ref[...], wd_ref[0])

    @pl.when(n == 0)
    def _():
        ys_ref[...] = jnp.zeros(ys_ref.shape, F32)


def _moe_ffn(texp, nvalid, xs, wg, wu, wd, tm):
    n_tiles = xs.shape[0] // tm
    wspec = lambda s: pl.BlockSpec((1,) + s, lambda i, te, nv: (te[i], 0, 0))
    row = pl.BlockSpec((tm, D_MODEL), lambda i, te, nv: (i, 0))
    return pl.pallas_call(
        _moe_ffn_kernel,
        grid_spec=pltpu.PrefetchScalarGridSpec(
            num_scalar_prefetch=2,
            grid=(n_tiles,),
            in_specs=[row, wspec((D_MODEL, D_FF)), wspec((D_MODEL, D_FF)), wspec((D_FF, D_MODEL))],
            out_specs=row,
            scratch_shapes=[pltpu.VMEM((tm, D_FF), BF16)],
        ),
        out_shape=jax.ShapeDtypeStruct(xs.shape, F32),
        compiler_params=_params("arbitrary"),
        name="moe_ffn",
    )(texp, nvalid, xs, wg, wu, wd)


def _moe_combine_kernel(pos_ref, x_ref, g_ref, ys_hbm, o_ref, ybuf, sem, *, tt):
    _start_row_copies(pos_ref, tt, lambda k, j, src: pltpu.make_async_copy(
        ys_hbm.at[pl.ds(src, 1), :], ybuf.at[k, pl.ds(j, 1), :], sem))
    for k in range(2):
        pltpu.make_async_copy(ys_hbm.at[pl.ds(0, tt), :], ybuf.at[k], sem).wait()
    g = g_ref[...]
    o_ref[...] = x_ref[...] + (g[:, 0:1] * ybuf[0] + g[:, 1:2] * ybuf[1])


def _moe_combine(pos_tab, x, gates, ys):
    t = x.shape[0]
    nsteps, _, tt2 = pos_tab.shape
    tt = tt2 // 2
    kern = functools.partial(_moe_combine_kernel, tt=tt)
    row = lambda n: pl.BlockSpec((tt, n), lambda i: (i, 0))
    return pl.pallas_call(
        kern,
        grid=(nsteps,),
        in_specs=[pl.BlockSpec((1, 1, tt2), lambda i: (i, 0, 0), memory_space=pltpu.SMEM),
                  row(D_MODEL), row(2), pl.BlockSpec(memory_space=pl.ANY)],
        out_specs=row(D_MODEL),
        out_shape=jax.ShapeDtypeStruct((t, D_MODEL), F32),
        scratch_shapes=[pltpu.VMEM((2, tt, D_MODEL), F32), pltpu.SemaphoreType.DMA(())],
        compiler_params=_params("arbitrary"),
        name="moe_combine",
    )(pos_tab, x, gates, ys)


def _moe(x, comb, mask, gain, wg, wu, wd):
    t = x.shape[0]
    tm = TOKEN_TILE if 2 * t >= N_EXPERTS * 4 * TOKEN_TILE else TOKEN_TILE // 2
    texp, nvalid, last_tile, pos_tab, gates, n_tiles = _moe_plan(mask, comb, tm)
    xs = _moe_dispatch(last_tile, pos_tab, x, gain, n_tiles * tm, tm)
    ys = _moe_ffn(texp, nvalid, xs, wg, wu, wd, tm)
    return _moe_combine(pos_tab, x, gates, ys)


def _pair_states(s):
    b = s.shape[0]
    s = s.reshape(b, RWKV_HEADS // 2, 2, RWKV_HEAD, RWKV_HEAD)
    z = jnp.zeros_like(s[:, :, 0])
    top = jnp.concatenate([s[:, :, 0], z], axis=-1)
    bot = jnp.concatenate([z, s[:, :, 1]], axis=-1)
    return jnp.concatenate([top, bot], axis=-2)


def _unpair_states(s):
    b = s.shape[0]
    s0 = s[:, :, :RWKV_HEAD, :RWKV_HEAD]
    s1 = s[:, :, RWKV_HEAD:, RWKV_HEAD:]
    return jnp.stack([s0, s1], axis=2).reshape(b, RWKV_HEADS, RWKV_HEAD, RWKV_HEAD)


def _trunk(x, ssm, conv, ck, cv, wkv, shift, w):
    b, l, _ = x.shape
    t = b * l
    ones_blk = w["ones_blk"]
    xf = x.reshape(t, D_MODEL)

    z, xbc, dt, q, k, v = _inproj(xf, w["norm_mix0"], w["wz"], w["wxbc"], w["wdt"], w["wq"], w["wk"], w["wv"],
                                  w["q_norm"], w["k_norm"], ones_blk)
    lp = l if l % SSD_CHUNK == 0 else 8
    seq = lambda a: a.reshape(b, l, a.shape[-1])
    padl = lambda a: a if lp == l else jnp.pad(a, ((0, 0), (0, lp - l), (0, 0)))
    y_ssd, ssm1, conv1 = _ssd(padl(seq(xbc)), padl(seq(dt)), padl(seq(z)), conv, ssm.reshape(b, SSD_WIDTH, SSD_STATE),
                              w["conv_w"], w["conv_b"], w["dt_bias"], w["a_log"], w["d_skip"], w["ssd_norm"], lv=l)
    y_ssd = y_ssd[:, :l].reshape(t, SSD_WIDTH)
    ssm1 = ssm1.reshape(b, SSD_HEADS, SSD_HEAD_DIM, SSD_STATE)
    if ck is None:
        o = _swa_prompt(seq(q), seq(k), seq(v), w["sinks"]).reshape(t, Q_DIM)
        k1 = seq(k)[:, -WINDOW:].reshape(b, WINDOW, ATT_KV_HEADS, ATT_HEAD_DIM)
        v1 = seq(v)[:, -WINDOW:].reshape(b, WINDOW, ATT_KV_HEADS, ATT_HEAD_DIM)
    else:
        wc = ck.shape[1]
        o, k1, v1 = _swa_cached(q, k, v, ck.reshape(b, wc, KV_DIM), cv.reshape(b, wc, KV_DIM), w["sinks"], l=l)
        k1 = k1.reshape(b, wc, ATT_KV_HEADS, ATT_HEAD_DIM)
        v1 = v1.reshape(b, wc, ATT_KV_HEADS, ATT_HEAD_DIM)
    x2 = _outproj_ffn(xf, y_ssd, o, w["wo_a"], w["wo_b"], w["norm_ffn0"], w["ffn_gate"], w["ffn_up"], w["ffn_down"])

    tm = min(TOKEN_TILE, t)
    if l >= tm:
        first = shift.reshape(b, 1, D_MODEL)
    else:
        first = jnp.repeat(shift, l, axis=0)
    r, lw, kx, vx, an, bb, gate, h1 = _rwkv_prep(
        x2, first, w["norm_mix1"], w["mu"], w["w0"], w["a0"], w["kk"], w["ka"], w["wr"], w["wkk"], w["wvv"],
        w["w1"], w["w2"], w["a1"], w["a2"], w["g1"], w["g2"], ones_blk, seq_len=l)
    shift1 = h1.reshape(b, D_MODEL) if l >= tm else h1.reshape(b, l, D_MODEL)[:, -1]
    lpr = -(-l // RWKV_CHUNK) * RWKV_CHUNK
    padr = lambda a: seq(a) if lpr == l else jnp.pad(seq(a), ((0, 0), (0, lpr - l), (0, 0)))
    y, s1 = _rwkv_scan(padr(r), padr(lw), padr(kx), padr(vx), padr(an), padr(bb), _pair_states(wkv))
    y = y[:, :l].reshape(t, D_MODEL)
    x3, comb, mask = _rwkv_out(x2, y, r, kx, vx, gate, w["ln_w"], w["ln_b"], w["rk"], w["wo"], ones_blk,
                               w["norm_ffn1"], w["router"])
    x4 = _moe(x3, comb, mask, w["norm_ffn1"], w["moe_gate"], w["moe_up"], w["moe_down"])
    return (x4.reshape(b, l, D_MODEL), ssm1[None], conv1[None], k1[None], v1[None],
            _unpair_states(s1)[None], shift1[None])


def kernel(x_prompt, x_sample, state_ssm, state_conv, cache_swa_k, cache_swa_v, state_wkv, state_shift, norm_mix, norm_ffn, w_in, conv_w, conv_b, dt_bias, a_log, d_skip, ssd_norm, q_norm, k_norm, attn_sinks, w_out, ffn_gate, ffn_up, ffn_down, rwkv_mu, rwkv_w0, rwkv_w1, rwkv_w2, rwkv_a0, rwkv_a1, rwkv_a2, rwkv_g1, rwkv_g2, rwkv_kk, rwkv_ka, rwkv_rk, rwkv_wr, rwkv_wk, rwkv_wv, rwkv_wo, rwkv_ln_w, rwkv_ln_b, moe_router, moe_gate, moe_up, moe_down):
    bf = lambda a: a.astype(BF16)
    row = lambda a: a.reshape(1, -1).astype(F32)
    padlane = lambda a: jnp.pad(a, ((0, 0), (0, LANES - a.shape[1])))
    wi = w_in[0]
    c0 = SSD_WIDTH
    c1 = c0 + CONV_DIM
    c2 = c1 + SSD_HEADS
    c3 = c2 + Q_DIM
    c4 = c3 + KV_DIM
    w = dict(
        ones_blk=_block_ones(),
        norm_mix0=row(norm_mix[0]), norm_mix1=row(norm_mix[1]), norm_ffn0=row(norm_ffn[0]), norm_ffn1=row(norm_ffn[1]),
        wz=bf(wi[:, :c0]), wxbc=bf(wi[:, c0:c1]), wdt=bf(padlane(wi[:, c1:c2])), wq=bf(wi[:, c2:c3]),
        wk=bf(wi[:, c3:c4]), wv=bf(wi[:, c4:]),
        q_norm=row(jnp.tile(q_norm[0], ATT_HEADS)), k_norm=row(jnp.tile(k_norm[0], ATT_KV_HEADS)),
        conv_w=conv_w[0], conv_b=row(conv_b[0]), dt_bias=padlane(row(dt_bias[0])), a_log=padlane(row(a_log[0])),
        d_skip=row(jnp.repeat(d_skip[0], SSD_HEAD_DIM)), ssd_norm=row(ssd_norm[0]), sinks=attn_sinks[0].astype(F32),
        wo_a=bf(w_out[0, :SSD_WIDTH]), wo_b=bf(w_out[0, SSD_WIDTH:]),
        ffn_gate=bf(ffn_gate[0]), ffn_up=bf(ffn_up[0]), ffn_down=bf(ffn_down[0]),
        mu=rwkv_mu[0], w0=row(rwkv_w0[0]), a0=row(rwkv_a0[0]), kk=row(rwkv_kk[0]), ka=row(rwkv_ka[0]),
        wr=bf(rwkv_wr[0]), wkk=bf(rwkv_wk[0]), wvv=bf(rwkv_wv[0]), wo=bf(rwkv_wo[0]),
        w1=bf(rwkv_w1[0]), w2=bf(rwkv_w2[0]), a1=bf(rwkv_a1[0]), a2=bf(rwkv_a2[0]),
        g1=bf(rwkv_g1[0]), g2=bf(rwkv_g2[0]),
        ln_w=row(rwkv_ln_w[0]), ln_b=row(rwkv_ln_b[0]), rk=row(rwkv_rk[0]),
        router=padlane(moe_router[0]), moe_gate=bf(moe_gate[0]), moe_up=bf(moe_up[0]), moe_down=bf(moe_down[0]),
    )
    bp = x_prompt.shape[0]
    z_ssm = jnp.zeros((bp,) + state_ssm.shape[2:], F32)
    z_conv = jnp.zeros((bp,) + state_conv.shape[2:], F32)
    z_wkv = jnp.zeros((bp,) + state_wkv.shape[2:], F32)
    z_shift = jnp.zeros((bp,) + state_shift.shape[2:], F32)
    outs_p = _trunk(x_prompt, z_ssm, z_conv, None, None, z_wkv, z_shift, w)
    outs_s = _trunk(x_sample, state_ssm[0], state_conv[0], cache_swa_k[0], cache_swa_v[0], state_wkv[0],
                    state_shift[0], w)
    return (outs_p[0], outs_s[0]) + tuple(outs_p[1:]) + tuple(outs_s[1:])
```

```python
import functools

import jax
import jax.numpy as jnp
from jax import lax
from jax.experimental import pallas as pl
from jax.experimental.pallas import tpu as pltpu

F32 = jnp.float32
BF16 = jnp.bfloat16

D_MODEL = 1024
SSD_HEAD_DIM = 64
SSD_HEADS = 16
SSD_GROUPS = 2
SSD_STATE = 128
SSD_CONV = 4
SSD_CHUNK = 128
SSD_WIDTH = 1024
CONV_DIM = SSD_WIDTH + 2 * SSD_GROUPS * SSD_STATE
ATT_HEAD_DIM = 64
ATT_HEADS = 16
ATT_KV_HEADS = 4
ATT_REP = ATT_HEADS // ATT_KV_HEADS
WINDOW = 128
Q_DIM = ATT_HEADS * ATT_HEAD_DIM
KV_DIM = ATT_KV_HEADS * ATT_HEAD_DIM
RWKV_HEAD = 64
RWKV_HEADS = 16
RWKV_GN_EPS = 64e-5
RWKV_CHUNK = 64
RWKV_SHORT_CHUNK = 16
D_FF = 2816
N_EXPERTS = 8
RMS_EPS = 1e-6
NEG_INF = -1e30

LANES = 128
MXU_DIM = 256
VMEM_LIMIT_BYTES = 56 * 1024 * 1024
TOKEN_TILE = 512
FF_CHUNK = MXU_DIM
HIGHEST = lax.Precision.HIGHEST


def _dot(a, b, precision=None):
    return jnp.dot(a, b, preferred_element_type=F32, precision=precision)


def _dot_nt(a, b, precision=None):
    return lax.dot_general(a, b, (((1,), (1,)), ((), ())), preferred_element_type=F32, precision=precision)


def _dot_tn(a, b, precision=None):
    return lax.dot_general(a, b, (((0,), (0,)), ((), ())), preferred_element_type=F32, precision=precision)


def _rms_rows(x, g):
    return x * lax.rsqrt(jnp.mean(x * x, axis=-1, keepdims=True) + RMS_EPS) * g


def _sigmoid(x):
    return 1.0 / (1.0 + jnp.exp(-x))


def _silu(x):
    return x * _sigmoid(x)


def _softplus(x):
    return jnp.maximum(x, 0.0) + jnp.log(1.0 + jnp.exp(-jnp.abs(x)))


def _const_spec(shape):
    zeros = (0,) * len(shape)
    return pl.BlockSpec(shape, lambda *_: zeros, pipeline_mode=pl.Buffered(1))


def _params(*semantics):
    return pltpu.CompilerParams(dimension_semantics=semantics, vmem_limit_bytes=VMEM_LIMIT_BYTES)


def _head_sums(t, ones_blk):
    return _dot(t.astype(BF16), ones_blk)


def _block_ones():
    r = jnp.arange(MXU_DIM) // ATT_HEAD_DIM
    return (r[:, None] == r[None, :]).astype(BF16)


def _inproj_kernel(x_ref, g_ref, wz_ref, wxbc_ref, wdt_ref, wq_ref, wk_ref, wv_ref, qn_ref, kn_ref, ones_ref,
                   z_ref, xbc_ref, dt_ref, q_ref, k_ref, v_ref):
    h = _rms_rows(x_ref[...], g_ref[...]).astype(BF16)
    ones_blk = ones_ref[...]

    def head_rms(t, gain):
        outs = []
        for c in range(t.shape[1] // MXU_DIM):
            tc = t[:, c * MXU_DIM:(c + 1) * MXU_DIM]
            ms = _head_sums(tc * tc, ones_blk) * (1.0 / ATT_HEAD_DIM)
            outs.append(tc * lax.rsqrt(ms + RMS_EPS))
        return jnp.concatenate(outs, axis=1) * gain

    z_ref[...] = _dot(h, wz_ref[...]).astype(z_ref.dtype)
    xbc_ref[...] = _dot(h, wxbc_ref[...])
    dt_ref[...] = _dot(h, wdt_ref[...])
    q_ref[...] = head_rms(_dot(h, wq_ref[...]), qn_ref[...]).astype(q_ref.dtype)
    k_ref[...] = head_rms(_dot(h, wk_ref[...]), kn_ref[...])
    v_ref[...] = _dot(h, wv_ref[...])


def _inproj(x, gain, wz, wxbc, wdt, wq, wk, wv, qn, kn, ones_blk):
    t = x.shape[0]
    tm = min(TOKEN_TILE, t)
    row = lambda n: pl.BlockSpec((tm, n), lambda i: (i, 0))
    return pl.pallas_call(
        _inproj_kernel,
        grid=(t // tm,),
        in_specs=[row(D_MODEL), _const_spec((1, D_MODEL)),
                  _const_spec(wz.shape), _const_spec(wxbc.shape), _const_spec(wdt.shape),
                  _const_spec(wq.shape), _const_spec(wk.shape), _const_spec(wv.shape),
                  _const_spec((1, Q_DIM)), _const_spec((1, KV_DIM)), _const_spec((MXU_DIM, MXU_DIM))],
        out_specs=[row(SSD_WIDTH), row(CONV_DIM), row(LANES), row(Q_DIM), row(KV_DIM), row(KV_DIM)],
        out_shape=[jax.ShapeDtypeStruct((t, SSD_WIDTH), BF16), jax.ShapeDtypeStruct((t, CONV_DIM), F32),
                   jax.ShapeDtypeStruct((t, LANES), F32), jax.ShapeDtypeStruct((t, Q_DIM), BF16),
                   jax.ShapeDtypeStruct((t, KV_DIM), F32), jax.ShapeDtypeStruct((t, KV_DIM), F32)],
        compiler_params=_params("arbitrary"),
        name="inproj",
    )(x, gain, wz, wxbc, wdt, wq, wk, wv, qn, kn, ones_blk)


def _ssd_kernel(xbc_ref, dt_ref, z_ref, conv0_ref, ssm0_ref, cw_ref, cb_ref, dtb_ref, alog_ref, dsk_ref, nrm_ref,
                y_ref, ssm1_ref, conv1_ref, xpad_ref, dtpad_ref, s_ref, *, q, lin, lv, nchunks):
    c = pl.program_id(1)
    n_pairs = SSD_HEADS // 2
    hist = SSD_CONV - 1
    base = 8 - hist

    @pl.when(c == 0)
    def _():
        xpad_ref[...] = jnp.zeros(xpad_ref.shape, F32)
        dtpad_ref[...] = jnp.zeros(dtpad_ref.shape, F32)
        xpad_ref[base:8, :] = conv0_ref[0]
        for jb in range(n_pairs):
            s_ref[:, jb * LANES:(jb + 1) * LANES] = ssm0_ref[0, jb * LANES:(jb + 1) * LANES, :].T

    xpad_ref[8:8 + lin, :] = xbc_ref[0]
    dtpad_ref[0:lin, :] = dt_ref[0]

    conv = cb_ref[...]
    for j in range(SSD_CONV):
        conv = conv + xpad_ref[base + j:base + j + q, :] * cw_ref[j:j + 1, :]
    tail = xpad_ref[base + lv:8 + lv, :]
    conv1_ref[0] = tail
    xpad_ref[base:8, :] = tail

    xc = _silu(conv)
    xs = xc[:, :SSD_WIDTH]
    bm = xc[:, SSD_WIDTH:SSD_WIDTH + SSD_GROUPS * SSD_STATE].astype(BF16)
    cm = xc[:, SSD_WIDTH + SSD_GROUPS * SSD_STATE:].astype(BF16)

    row_i = lax.broadcasted_iota(jnp.int32, (q, LANES), 0)
    col_i = lax.broadcasted_iota(jnp.int32, (q, LANES), 1)
    first_half = col_i < SSD_HEAD_DIM
    dt = jnp.where(row_i < lv, _softplus(dtpad_ref[...] + dtb_ref[...]), 0.0)
    a = dt * (-jnp.exp(alog_ref[...]))
    r_q = lax.broadcasted_iota(jnp.int32, (q, q), 0)
    c_q = lax.broadcasted_iota(jnp.int32, (q, q), 1)
    causal = r_q >= c_q
    a_cs = _dot(causal.astype(F32), a, precision=HIGHEST)
    a_cs_t = a_cs.T
    ea = jnp.exp(a_cs)
    dte = jnp.exp(a_cs[q - 1:q, :] - a_cs)

    def colb(arr, h):
        return jnp.broadcast_to(arr[:, h:h + 1], (q, LANES))

    def pairb(arr, h0):
        return jnp.where(first_half, colb(arr, h0), colb(arr, h0 + 1))

    cbs = []
    for g in range(SSD_GROUPS):
        sl = slice(g * SSD_STATE, (g + 1) * SSD_STATE)
        cbs.append(_dot_nt(cm[:, sl], bm[:, sl]))

    ys = []
    for j in range(n_pairs):
        g = (2 * j) // (SSD_HEADS // SSD_GROUPS)
        gsl = slice(g * SSD_STATE, (g + 1) * SSD_STATE)
        psl = slice(j * LANES, (j + 1) * LANES)
        xs_p = xs[:, psl]
        xdt = xs_p * pairb(dt, 2 * j)
        xdt_b = xdt.astype(BF16)
        yd = []
        for hh in (2 * j, 2 * j + 1):
            diff = colb(a_cs, hh) - a_cs_t[hh:hh + 1, :]
            lmat = jnp.exp(jnp.where(causal, diff, NEG_INF))
            yd.append(_dot((cbs[g] * lmat).astype(BF16), xdt_b))
        y_diag = jnp.where(first_half, yd[0], yd[1])
        ea_p = pairb(ea, 2 * j)
        s_old = s_ref[:, psl]
        y_off = _dot(cm[:, gsl], s_old.astype(BF16)) * ea_p
        s_ref[:, psl] = s_old * ea_p[q - 1:q, :] + _dot_tn(bm[:, gsl], (xdt * pairb(dte, 2 * j)).astype(BF16))
        ys.append(y_diag + y_off + xs_p * dsk_ref[:, psl])
    y = jnp.concatenate(ys, axis=1)[0:lin]

    y = y * _silu(z_ref[0].astype(F32))
    gw = SSD_WIDTH // SSD_GROUPS
    outs = []
    for g in range(SSD_GROUPS):
        yg = y[:, g * gw:(g + 1) * gw]
        outs.append(yg * lax.rsqrt(jnp.mean(yg * yg, axis=-1, keepdims=True) + RMS_EPS))
    y_ref[0] = (jnp.concatenate(outs, axis=1) * nrm_ref[...]).astype(y_ref.dtype)

    @pl.when(c == nchunks - 1)
    def _():
        for jb in range(n_pairs):
            ssm1_ref[0, jb * LANES:(jb + 1) * LANES, :] = s_ref[:, jb * LANES:(jb + 1) * LANES].T


def _ssd(xbc, dt, z, conv0, ssm0, cw, cb, dtb, alog, dsk, nrm, *, lv):
    b, lp, _ = xbc.shape
    q = SSD_CHUNK
    lin = min(q, lp)
    nchunks = lp // lin
    lv_chunk = min(lv, lin)
    kern = functools.partial(_ssd_kernel, q=q, lin=lin, lv=lv_chunk, nchunks=nchunks)
    seq = lambda n: pl.BlockSpec((1, lin, n), lambda i, c: (i, c, 0))
    per_b = lambda s: pl.BlockSpec((1,) + s, lambda i, c: (i, 0, 0))
    return pl.pallas_call(
        kern,
        grid=(b, nchunks),
        in_specs=[seq(CONV_DIM), seq(LANES), seq(SSD_WIDTH),
                  per_b((SSD_CONV - 1, CONV_DIM)), per_b((SSD_WIDTH, SSD_STATE)),
                  _const_spec((SSD_CONV, CONV_DIM)), _const_spec((1, CONV_DIM)), _const_spec((1, LANES)),
                  _const_spec((1, LANES)), _const_spec((1, SSD_WIDTH)), _const_spec((1, SSD_WIDTH))],
        out_specs=[seq(SSD_WIDTH), per_b((SSD_WIDTH, SSD_STATE)), per_b((SSD_CONV - 1, CONV_DIM))],
        out_shape=[jax.ShapeDtypeStruct((b, lp, SSD_WIDTH), BF16),
                   jax.ShapeDtypeStruct((b, SSD_WIDTH, SSD_STATE), F32),
                   jax.ShapeDtypeStruct((b, SSD_CONV - 1, CONV_DIM), F32)],
        scratch_shapes=[pltpu.VMEM((q + 8, CONV_DIM), F32), pltpu.VMEM((q, LANES), F32),
                        pltpu.VMEM((SSD_STATE, SSD_WIDTH), F32)],
        compiler_params=_params("arbitrary", "arbitrary"),
        name="ssd",
    )(xbc, dt, z, conv0, ssm0, cw, cb, dtb, alog, dsk, nrm)


def _sink_softmax(s, allowed, sink_col):
    s = jnp.where(allowed, s, NEG_INF)
    m = jnp.maximum(jnp.max(s, axis=-1, keepdims=True), sink_col)
    return jnp.exp(s - m).astype(BF16), jnp.exp(sink_col - m)


def _gqa_blocks(blocks, sink_ref, allowed, nq, group):
    d = ATT_HEAD_DIM
    sinks = [jnp.concatenate([jnp.full((nq, 1), sink_ref[ATT_REP * j + r], F32) for r in range(ATT_REP)], axis=0)
             for j in range(ATT_KV_HEADS)]
    problems = [(bi, j) for bi in range(len(blocks)) for j in range(ATT_KV_HEADS)]
    ones_k = jnp.ones((blocks[0][1].shape[0], d), BF16)
    outs = {}
    for g0 in range(0, len(problems), group):
        grp = problems[g0:g0 + group]
        scores = []
        for bi, j in grp:
            qb, kk, _ = blocks[bi]
            qj = jnp.concatenate(
                [qb[:, (ATT_REP * j + r) * d:(ATT_REP * j + r + 1) * d] for r in range(ATT_REP)], axis=0)
            scores.append(_dot_nt(qj * (d ** -0.5), kk[:, j * d:(j + 1) * d]))
        probs = [_sink_softmax(s, allowed, sinks[j]) for s, (_, j) in zip(scores, grp)]
        for (p, sink_term), (bi, j) in zip(probs, grp):
            denom = _dot(p, ones_k) + sink_term
            outs[bi, j] = _dot(p, blocks[bi][2][:, j * d:(j + 1) * d]) / denom
    return [jnp.concatenate([outs[bi, j][r * nq:(r + 1) * nq] for j in range(ATT_KV_HEADS) for r in range(ATT_REP)],
                            axis=1) for bi in range(len(blocks))]


def _swa_prompt_kernel(sink_ref, q_ref, kp_ref, kc_ref, vp_ref, vc_ref, o_ref):
    i = pl.program_id(1)
    w = WINDOW
    kk = jnp.concatenate([kp_ref[0], kc_ref[0]], axis=0).astype(BF16)
    vv = jnp.concatenate([vp_ref[0], vc_ref[0]], axis=0).astype(BF16)
    rows = lax.broadcasted_iota(jnp.int32, (ATT_REP * w, 2 * w), 0) % w
    cols = lax.broadcasted_iota(jnp.int32, (ATT_REP * w, 2 * w), 1)
    rel = rows + w - cols
    allowed = (rel >= 0) & (rel <= w) & ((cols >= w) | (i > 0))
    o_ref[0] = _gqa_blocks([(q_ref[0], kk, vv)], sink_ref, allowed, w, group=ATT_KV_HEADS)[0].astype(o_ref.dtype)


def _swa_prompt(q, k, v, sinks):
    b, l, _ = q.shape
    nb = l // WINDOW
    cur = lambda n: pl.BlockSpec((1, WINDOW, n), lambda i, c: (i, c, 0))
    prev = lambda n: pl.BlockSpec((1, WINDOW, n), lambda i, c: (i, jnp.maximum(c - 1, 0), 0))
    return pl.pallas_call(
        _swa_prompt_kernel,
        grid=(b, nb),
        in_specs=[pl.BlockSpec(memory_space=pltpu.SMEM),
                  cur(Q_DIM), prev(KV_DIM), cur(KV_DIM), prev(KV_DIM), cur(KV_DIM)],
        out_specs=cur(Q_DIM),
        out_shape=jax.ShapeDtypeStruct((b, l, Q_DIM), BF16),
        compiler_params=_params("arbitrary", "arbitrary"),
        name="swa_prompt",
    )(sinks, q, k, k, v, v)


def _swa_cached_kernel(sink_ref, q_ref, k_ref, v_ref, kbuf_ref, vbuf_ref, o_ref, k1_ref, v1_ref, *, nbatch, l):
    wc = kbuf_ref.shape[1]
    pad = 8 - l
    nk = wc + 8
    rows = lax.broadcasted_iota(jnp.int32, (ATT_REP * l, nk), 0) % l
    cols = lax.broadcasted_iota(jnp.int32, (ATT_REP * l, nk), 1)
    rel = rows + wc - cols
    allowed = (rel >= 0) & (rel <= WINDOW)
    zpad = jnp.zeros((pad, KV_DIM), F32)
    qall = q_ref[...].astype(F32)
    blocks = []
    for bi in range(nbatch):
        kn = k_ref[bi * l:(bi + 1) * l, :]
        vn = v_ref[bi * l:(bi + 1) * l, :]
        kc = jnp.concatenate([kbuf_ref[bi], kn, zpad], axis=0).astype(BF16)
        vc = jnp.concatenate([vbuf_ref[bi], vn, zpad], axis=0).astype(BF16)
        blocks.append((qall[bi * l:(bi + 1) * l, :].astype(BF16), kc, vc))
        k1_ref[bi, 0:wc - l, :] = kbuf_ref[bi, l:wc, :]
        k1_ref[bi, wc - l:wc, :] = kn
        v1_ref[bi, 0:wc - l, :] = vbuf_ref[bi, l:wc, :]
        v1_ref[bi, wc - l:wc, :] = vn
    o_ref[...] = jnp.concatenate(_gqa_blocks(blocks, sink_ref, allowed, l, group=ATT_KV_HEADS * nbatch), axis=0).astype(o_ref.dtype)


def _swa_cached(q, k, v, kbuf, vbuf, sinks, *, l):
    b, wc, _ = kbuf.shape
    nbatch = 8
    kern = functools.partial(_swa_cached_kernel, nbatch=nbatch, l=l)
    rows = lambda n: pl.BlockSpec((nbatch * l, n), lambda i: (i, 0))
    cache = pl.BlockSpec((nbatch, wc, KV_DIM), lambda i: (i, 0, 0))
    return pl.pallas_call(
        kern,
        grid=(b // nbatch,),
        in_specs=[pl.BlockSpec(memory_space=pltpu.SMEM), rows(Q_DIM), rows(KV_DIM), rows(KV_DIM), cache, cache],
        out_specs=[rows(Q_DIM), cache, cache],
        out_shape=[jax.ShapeDtypeStruct((b * l, Q_DIM), BF16),
                   jax.ShapeDtypeStruct((b, wc, KV_DIM), F32), jax.ShapeDtypeStruct((b, wc, KV_DIM), F32)],
        compiler_params=_params("arbitrary"),
        name="swa_cached",
    )(sinks, q, k, v, kbuf, vbuf)


def _swiglu_act(h, wg_ref, wu_ref, act_ref):
    for c in range(D_FF // FF_CHUNK):
        sl = slice(c * FF_CHUNK, (c + 1) * FF_CHUNK)
        act_ref[:, sl] = (_silu(_dot(h, wg_ref[:, sl])) * _dot(h, wu_ref[:, sl])).astype(BF16)


def _outproj_ffn_kernel(x_ref, ya_ref, yb_ref, woa_ref, wob_ref, g_ref, wg_ref, wu_ref, wd_ref, o_ref, act_ref):
    x1 = x_ref[...] + _dot(ya_ref[...], woa_ref[...]) + _dot(yb_ref[...], wob_ref[...])
    h = _rms_rows(x1, g_ref[...]).astype(BF16)
    _swiglu_act(h, wg_ref, wu_ref, act_ref)
    o_ref[...] = x1 + _dot(act_ref[...], wd_ref[...])


def _outproj_ffn(x, ya, yb, woa, wob, gain, wg, wu, wd):
    t = x.shape[0]
    tm = min(TOKEN_TILE, t)
    row = lambda n: pl.BlockSpec((tm, n), lambda i: (i, 0))
    return pl.pallas_call(
        _outproj_ffn_kernel,
        grid=(t // tm,),
        in_specs=[row(D_MODEL), row(SSD_WIDTH), row(Q_DIM), _const_spec(woa.shape), _const_spec(wob.shape),
                  _const_spec((1, D_MODEL)), _const_spec(wg.shape), _const_spec(wu.shape), _const_spec(wd.shape)],
        out_specs=row(D_MODEL),
        out_shape=jax.ShapeDtypeStruct((t, D_MODEL), F32),
        scratch_shapes=[pltpu.VMEM((tm, D_FF), BF16)],
        compiler_params=_params("arbitrary"),
        name="outproj_ffn",
    )(x, ya, yb, woa, wob, gain, wg, wu, wd)


def _rwkv_prep_kernel(x_ref, first_ref, g_ref, mu_ref, w0_ref, a0_ref, kkc_ref, ka_ref,
                      wr_ref, wk_ref, wv_ref, w1_ref, w2_ref, a1_ref, a2_ref, g1_ref, g2_ref, ones_ref,
                      r_ref, lw_ref, k_ref, v_ref, an_ref, b_ref, gate_ref, h_ref, hs_ref, *, tm, seq_len):
    i = pl.program_id(0)
    h = _rms_rows(x_ref[...], g_ref[...])
    tiles_per_seq = max(seq_len // tm, 1)

    if seq_len >= tm:
        h_ref[0] = h[tm - 1:tm, :]

        @pl.when(i % tiles_per_seq == 0)
        def _():
            hs_ref[7:8, :] = first_ref[0]
    else:
        h_ref[...] = h
        hs_ref[0:8, :] = jnp.zeros((8, D_MODEL), F32)
    hs_ref[8:8 + tm, :] = h
    prev = hs_ref[7:7 + tm, :]
    if seq_len < tm:
        rows = lax.broadcasted_iota(jnp.int32, (tm, D_MODEL), 0)
        prev = jnp.where(rows % seq_len == 0, first_ref[...], prev)
    else:
        hs_ref[7:8, :] = h[tm - 1:tm, :]
    xx = prev - h
    mix = lambda n: (h + xx * mu_ref[n:n + 1, :]).astype(BF16)
    r = _dot(mix(0), wr_ref[...])
    k = _dot(mix(2), wk_ref[...])
    v = _dot(mix(3), wv_ref[...])
    w_lora = _dot(jnp.tanh(_dot(mix(1), w1_ref[...])).astype(BF16), w2_ref[...])
    a_lora = _dot(_dot(mix(4), a1_ref[...]).astype(BF16), a2_ref[...])
    gate = _dot(_sigmoid(_dot(mix(5), g1_ref[...])).astype(BF16), g2_ref[...])
    w_log = -_softplus(-(w0_ref[...] + w_lora)) - 0.5
    a_sig = _sigmoid(a0_ref[...] + a_lora)
    kk = k * kkc_ref[...]
    ones_blk = ones_ref[...]
    parts = []
    for c in range(D_MODEL // MXU_DIM):
        kc = kk[:, c * MXU_DIM:(c + 1) * MXU_DIM]
        nrm = jnp.maximum(jnp.sqrt(_head_sums(kc * kc, ones_blk)), 1e-12)
        parts.append(kc / nrm)
    kk = jnp.concatenate(parts, axis=1)
    r_ref[...] = r.astype(r_ref.dtype)
    lw_ref[...] = -jnp.exp(w_log)
    k_ref[...] = (k * (1.0 + (a_sig - 1.0) * ka_ref[...])).astype(k_ref.dtype)
    v_ref[...] = v.astype(v_ref.dtype)
    an_ref[...] = (-kk).astype(an_ref.dtype)
    b_ref[...] = (kk * a_sig).astype(b_ref.dtype)
    gate_ref[...] = gate.astype(gate_ref.dtype)


def _rwkv_prep(x, first, gain, mu, w0, a0, kkc, ka, wr, wk, wv, w1, w2, a1, a2, g1, g2, ones_blk, *, seq_len):
    t = x.shape[0]
    tm = min(TOKEN_TILE, t)
    kern = functools.partial(_rwkv_prep_kernel, tm=tm, seq_len=seq_len)
    row = lambda n: pl.BlockSpec((tm, n), lambda i: (i, 0))
    if seq_len >= tm:
        tiles_per_seq = seq_len // tm
        first_spec = pl.BlockSpec((1, 1, D_MODEL), lambda i: (i // tiles_per_seq, 0, 0))
        last_spec = first_spec
        last_shape = jax.ShapeDtypeStruct((t // seq_len, 1, D_MODEL), F32)
    else:
        first_spec = row(D_MODEL)
        last_spec = row(D_MODEL)
        last_shape = jax.ShapeDtypeStruct((t, D_MODEL), F32)
    vec = _const_spec((1, D_MODEL))
    wide = lambda dt: jax.ShapeDtypeStruct((t, D_MODEL), dt)
    outs = [wide(BF16), wide(F32), wide(BF16), wide(BF16), wide(BF16), wide(BF16), wide(BF16), last_shape]
    return pl.pallas_call(
        kern,
        grid=(t // tm,),
        in_specs=[row(D_MODEL), first_spec, vec, _const_spec((6, D_MODEL)), vec, vec, vec, vec,
                  _const_spec(wr.shape), _const_spec(wk.shape), _const_spec(wv.shape),
                  _const_spec(w1.shape), _const_spec(w2.shape), _const_spec(a1.shape), _const_spec(a2.shape),
                  _const_spec(g1.shape), _const_spec(g2.shape), _const_spec((MXU_DIM, MXU_DIM))],
        out_specs=[row(D_MODEL)] * 7 + [last_spec],
        out_shape=outs,
        scratch_shapes=[pltpu.VMEM((tm + 8, D_MODEL), F32)],
        compiler_params=_params("arbitrary"),
        name="rwkv_prep",
    )(x, first, gain, mu, w0, a0, kkc, ka, wr, wk, wv, w1, w2, a1, a2, g1, g2, ones_blk)


def _rwkv_scan_kernel(r_ref, lw_ref, k_ref, v_ref, a_ref, b_ref, s0_ref, y_ref, s1_ref, s_ref, *, c, nsub, nsteps):
    step = pl.program_id(1)
    n_pairs = RWKV_HEADS // 2
    two_c = 2 * c
    levels = c.bit_length() - 1

    @pl.when(step == 0)
    def _():
        s_ref[...] = s0_ref[0]

    lane = lax.broadcasted_iota(jnp.int32, (c, LANES), 1)
    m_a = lane < RWKV_HEAD
    r_i = lax.broadcasted_iota(jnp.int32, (two_c, two_c), 0)
    c_i = lax.broadcasted_iota(jnp.int32, (two_c, two_c), 1)
    strict = r_i > c_i
    s_r = lax.broadcasted_iota(jnp.int32, (LANES, LANES), 0)
    s_c = lax.broadcasted_iota(jnp.int32, (LANES, LANES), 1)
    blk = (s_r < RWKV_HEAD) == (s_c < RWKV_HEAD)
    r_i2 = lax.broadcasted_iota(jnp.int32, (two_c, 2 * two_c), 0)
    c_i2 = lax.broadcasted_iota(jnp.int32, (two_c, 2 * two_c), 1) % two_c
    incl2 = r_i2 >= c_i2
    t_r = lax.broadcasted_iota(jnp.int32, (c, c), 0)
    t_c = lax.broadcasted_iota(jnp.int32, (c, c), 1)
    tri = (t_r >= t_c).astype(F32)

    def stack(x):
        return jnp.concatenate([jnp.where(m_a, x, 0.0), jnp.where(m_a, 0.0, x)], axis=0)

    def sub_chunk(ci, carry):
        rows = pl.ds(pl.multiple_of(ci * c, c), c)
        lw = lw_ref[0, rows, :]
        cl = _dot(tri, lw, precision=HIGHEST)
        cl_end = cl[c - 1:c, :]
        e_neg = jnp.exp(-cl)
        e_end = jnp.exp(cl_end - cl)
        w_end = jnp.exp(cl_end)
        aa = a_ref[0, rows, :].astype(F32) * jnp.exp(cl - lw)
        rr = r_ref[0, rows, :].astype(F32) * jnp.exp(cl)
        bb = b_ref[0, rows, :].astype(F32)
        kk = k_ref[0, rows, :].astype(F32)
        vv = v_ref[0, rows, :].astype(F32)
        pairs = range(n_pairs)
        sl = [slice(p * LANES, (p + 1) * LANES) for p in pairs]
        lhs = [jnp.concatenate([stack(aa[:, sl[p]]), stack(rr[:, sl[p]])], axis=0).astype(BF16) for p in pairs]
        rhs = [jnp.concatenate([stack((bb * e_neg)[:, sl[p]]), stack((kk * e_neg)[:, sl[p]])], axis=0).astype(BF16)
               for p in pairs]
        bk = [jnp.concatenate([stack((bb * e_end)[:, sl[p]]), stack((kk * e_end)[:, sl[p]])], axis=0).astype(BF16)
              for p in pairs]
        v_s = [stack(vv[:, sl[p]]) for p in pairs]
        s_old = [s_ref[p] for p in pairs]
        gmat = [_dot_nt(lhs[p], rhs[p]) for p in pairs]
        ph = [_dot_nt(lhs[p], s_old[p].astype(BF16)) for p in pairs]
        npow = [jnp.where(strict, gmat[p][:two_c, :two_c], 0.0).astype(BF16) for p in pairs]
        a_ak = [jnp.where(strict, gmat[p][:two_c, two_c:], 0.0).astype(BF16) for p in pairs]
        a_r = [jnp.where(incl2, gmat[p][two_c:, :], 0.0).astype(BF16) for p in pairs]
        u = [ph[p][:two_c] + _dot(a_ak[p], v_s[p].astype(BF16)) for p in pairs]
        for lvl in range(levels - 1):
            prod = [_dot(npow[p], jnp.concatenate([u[p].astype(BF16), npow[p]], axis=1)) for p in pairs]
            u = [u[p] + prod[p][:, :LANES] for p in pairs]
            npow = [prod[p][:, LANES:].astype(BF16) for p in pairs]
        u = [u[p] + _dot(npow[p], u[p].astype(BF16)) for p in pairs]
        uv = [jnp.concatenate([u[p], v_s[p]], axis=0).astype(BF16) for p in pairs]
        y_s = [ph[p][two_c:] + _dot(a_r[p], uv[p]) for p in pairs]
        y_ref[0, rows, :] = jnp.concatenate([y_s[p][:c] + y_s[p][c:] for p in pairs], axis=1).astype(y_ref.dtype)
        for p in pairs:
            s_new = s_old[p] * w_end[:, sl[p]] + _dot_tn(uv[p], bk[p])
            s_ref[p] = jnp.where(blk, s_new, 0.0)
        return carry

    lax.fori_loop(0, nsub, sub_chunk, 0)

    @pl.when(step == nsteps - 1)
    def _():
        s1_ref[0] = s_ref[...]


def _rwkv_scan(r, lw, k, v, an, b, s0):
    bsz, lp, _ = r.shape
    c = min(RWKV_CHUNK, lp)
    rows = min(lp, 512)
    nsteps = lp // rows
    nsub = rows // c
    kern = functools.partial(_rwkv_scan_kernel, c=c, nsub=nsub, nsteps=nsteps)
    seq = pl.BlockSpec((1, rows, D_MODEL), lambda i, s: (i, s, 0))
    st = pl.BlockSpec((1, RWKV_HEADS // 2, LANES, LANES), lambda i, s: (i, 0, 0, 0))
    return pl.pallas_call(
        kern,
        grid=(bsz, nsteps),
        in_specs=[seq] * 6 + [st],
        out_specs=[seq, st],
        out_shape=[jax.ShapeDtypeStruct((bsz, lp, D_MODEL), BF16),
                   jax.ShapeDtypeStruct((bsz, RWKV_HEADS // 2, LANES, LANES), F32)],
        scratch_shapes=[pltpu.VMEM((RWKV_HEADS // 2, LANES, LANES), F32)],
        compiler_params=_params("arbitrary", "arbitrary"),
        name="rwkv_scan",
    )(r, lw, k, v, an, b, s0)


def _route_top2(h, router):
    logits = _dot(h, router, precision=HIGHEST)
    lane = lax.broadcasted_iota(jnp.int32, logits.shape, 1)
    logits = jnp.where(lane < N_EXPERTS, logits, -jnp.inf)
    m1 = jnp.max(logits, axis=-1, keepdims=True)
    i1 = jnp.min(jnp.where(logits == m1, lane, LANES), axis=-1, keepdims=True)
    rest = jnp.where(lane == i1, -jnp.inf, logits)
    m2 = jnp.max(rest, axis=-1, keepdims=True)
    i2 = jnp.min(jnp.where(rest == m2, lane, LANES), axis=-1, keepdims=True)
    e2 = jnp.exp(m2 - m1)
    g1 = 1.0 / (1.0 + e2)
    g2 = e2 / (1.0 + e2)
    comb = jnp.where(lane == i1, g1, 0.0) + jnp.where(lane == i2, g2, 0.0)
    mask = jnp.where((lane == i1) | (lane == i2), 1.0, 0.0)
    return comb, mask


def _rwkv_out_kernel(x_ref, y_ref, r_ref, k_ref, v_ref, gate_ref, lnw_ref, lnb_ref, rk_ref, wo_ref, ones_ref, o_ref):
    ones_blk = ones_ref[...]
    inv = 1.0 / RWKV_HEAD
    parts = []
    for c in range(D_MODEL // MXU_DIM):
        sl = slice(c * MXU_DIM, (c + 1) * MXU_DIM)
        yh = y_ref[:, sl]
        y = yh.astype(F32)
        yc = y - _dot(yh, ones_blk) * inv
        var = _head_sums(yc * yc, ones_blk) * inv
        yn = yc * lax.rsqrt(var + RWKV_GN_EPS) * lnw_ref[:, sl] + lnb_ref[:, sl]
        rk = r_ref[:, sl].astype(F32) * k_ref[:, sl].astype(F32) * rk_ref[:, sl]
        rkh = rk.astype(BF16)
        bonus = _dot(rkh, ones_blk) + _head_sums(rk - rkh.astype(F32), ones_blk)
        parts.append(yn + bonus * v_ref[:, sl].astype(F32))
    out = jnp.concatenate(parts, axis=1) * gate_ref[...].astype(F32)
    o_ref[...] = x_ref[...] + _dot(out.astype(BF16), wo_ref[...])


def _rwkv_out(x, y, r, k, v, gate, lnw, lnb, rk, wo, ones_blk):
    t = x.shape[0]
    tm = min(TOKEN_TILE, t)
    row = pl.BlockSpec((tm, D_MODEL), lambda i: (i, 0))
    vec = _const_spec((1, D_MODEL))
    return pl.pallas_call(
        _rwkv_out_kernel,
        grid=(t // tm,),
        in_specs=[row] * 6 + [vec, vec, vec, _const_spec(wo.shape), _const_spec((MXU_DIM, MXU_DIM))],
        out_specs=row,
        out_shape=jax.ShapeDtypeStruct((t, D_MODEL), F32),
        compiler_params=_params("arbitrary"),
        name="rwkv_out",
    )(x, y, r, k, v, gate, lnw, lnb, rk, wo, ones_blk)


def _moe_route_kernel(x_ref, g_ref, router_ref, comb_ref, mask_ref):
    comb_ref[...], mask_ref[...] = _route_top2(_rms_rows(x_ref[...], g_ref[...]), router_ref[...])


def _moe_route(x, gain, router):
    t = x.shape[0]
    tm = min(TOKEN_TILE, t)
    row = lambda n: pl.BlockSpec((tm, n), lambda i: (i, 0))
    return pl.pallas_call(
        _moe_route_kernel,
        grid=(t // tm,),
        in_specs=[row(D_MODEL), _const_spec((1, D_MODEL)), _const_spec((D_MODEL, LANES))],
        out_specs=[row(LANES), row(LANES)],
        out_shape=[jax.ShapeDtypeStruct((t, LANES), F32), jax.ShapeDtypeStruct((t, LANES), F32)],
        compiler_params=_params("arbitrary"),
        name="moe_route",
    )(x, gain, router)


def _moe_plan(mask, comb, tm):
    m = mask[:, :N_EXPERTS].astype(jnp.int32)
    t = m.shape[0]
    tt = min(TOKEN_TILE, t)
    n_tiles = (2 * t) // tm + N_EXPERTS
    rank = jnp.cumsum(m, axis=0)
    counts = rank[-1]
    tiles_e = (counts + tm - 1) // tm
    tile_end = jnp.cumsum(tiles_e)
    tile_start = tile_end - tiles_e
    pos = (tile_start * tm)[None, :] + rank - 1
    slot = jnp.cumsum(m, axis=1) - 1
    c8 = comb[:, :N_EXPERTS]
    pick = [(slot == k) & (m > 0) for k in range(2)]
    pos_k = jnp.stack([jnp.sum(jnp.where(pk, pos, 0), axis=1) for pk in pick], axis=0)
    gates = jnp.stack([jnp.sum(jnp.where(pk, c8, 0.0), axis=1) for pk in pick], axis=1)
    pos_tab = pos_k.reshape(2, t // tt, tt).transpose(1, 0, 2).reshape(t // tt, 1, 2 * tt)
    tile_ids = jnp.arange(n_tiles, dtype=jnp.int32)
    texp = jnp.sum((tile_ids[:, None] >= tile_end[None, :]).astype(jnp.int32), axis=1)
    texp_c = jnp.minimum(texp, N_EXPERTS - 1)
    nvalid = jnp.clip(counts[texp_c] - (tile_ids - tile_start[texp_c]) * tm, 0, tm)
    nvalid = jnp.where(texp < N_EXPERTS, nvalid, 0).astype(jnp.int32)
    last_tile = jnp.concatenate([jnp.where(tiles_e > 0, tile_end - 1, -1), tile_end[-1:]]).astype(jnp.int32)
    return texp_c.astype(jnp.int32), nvalid, last_tile, pos_tab.astype(jnp.int32), gates, n_tiles


def _start_row_copies(pos_ref, tt, make):
    def body(j, c):
        for k in range(2):
            make(k, j, pos_ref[0, 0, k * tt + j]).start(priority=k)
        return c
    lax.fori_loop(0, tt, body, 0, unroll=8)


def _moe_dispatch_kernel(last_ref, pos_ref, x_ref, g_ref, xs_hbm, hbuf, zbuf, sem, zsem, *, tt, tm, n_tiles, nsteps):
    i = pl.program_id(0)

    def zero_tile(tile):
        return pltpu.make_async_copy(zbuf, xs_hbm.at[pl.ds(pl.multiple_of(tile * tm, tm), tm), :], zsem)

    @pl.when(i == 0)
    def _():
        zbuf[...] = jnp.zeros(zbuf.shape, F32)
        for e in range(N_EXPERTS):
            @pl.when(last_ref[e] >= 0)
            def _():
                zero_tile(last_ref[e]).start()
        for e in range(N_EXPERTS):
            @pl.when(last_ref[e] >= 0)
            def _():
                zero_tile(last_ref[e]).wait()

        def clear_unused(tile, c):
            zero_tile(tile).start()
            zero_tile(tile).wait()
            return c
        lax.fori_loop(last_ref[N_EXPERTS], n_tiles, clear_unused, 0)

    def drain(s):
        for _ in range(2):
            pltpu.make_async_copy(hbuf.at[s], xs_hbm.at[pl.ds(0, tt), :], sem.at[s]).wait()

    for s in range(2):
        @pl.when((i % 2 == s) & (i >= 2))
        def _():
            drain(s)

        @pl.when(i % 2 == s)
        def _():
            hbuf[s] = _rms_rows(x_ref[...], g_ref[...])
            _start_row_copies(pos_ref, tt, lambda k, j, dst: pltpu.make_async_copy(
                hbuf.at[s, pl.ds(j, 1), :], xs_hbm.at[pl.ds(dst, 1), :], sem.at[s]))

    @pl.when(i == nsteps - 1)
    def _():
        for s in range(min(2, nsteps)):
            drain(s)


def _moe_dispatch(last_tile, pos_tab, x, gain, n_rows, tm):
    nsteps, _, tt2 = pos_tab.shape
    tt = tt2 // 2
    kern = functools.partial(_moe_dispatch_kernel, tt=tt, tm=tm, n_tiles=n_rows // tm, nsteps=nsteps)
    return pl.pallas_call(
        kern,
        grid_spec=pltpu.PrefetchScalarGridSpec(
            num_scalar_prefetch=1,
            grid=(nsteps,),
            in_specs=[pl.BlockSpec((1, 1, tt2), lambda i, lt: (i, 0, 0), memory_space=pltpu.SMEM),
                      pl.BlockSpec((tt, D_MODEL), lambda i, lt: (i, 0)),
                      pl.BlockSpec((1, D_MODEL), lambda i, lt: (0, 0))],
            out_specs=pl.BlockSpec(memory_space=pl.ANY),
            scratch_shapes=[pltpu.VMEM((2, tt, D_MODEL), F32), pltpu.VMEM((tm, D_MODEL), F32),
                            pltpu.SemaphoreType.DMA((2,)), pltpu.SemaphoreType.DMA(())],
        ),
        out_shape=jax.ShapeDtypeStruct((n_rows, D_MODEL), F32),
        compiler_params=_params("arbitrary"),
        name="moe_dispatch",
    )(last_tile, pos_tab, x, gain)


def _moe_ffn_kernel(texp_ref, nvalid_ref, xs_ref, wg_ref, wu_ref, wd_ref, ys_ref, act_ref):
    n = nvalid_ref[pl.program_id(0)]

    @pl.when(n > 0)
    def _():
        _swiglu_act(xs_ref[...].astype(BF16), wg_ref.at[0], wu_ref.at[0], act_ref)
        ys_ref[...] = _dot(act_ref[...], wd_ref[0])

    @pl.when(n == 0)
    def _():
        ys_ref[...] = jnp.zeros(ys_ref.shape, F32)


def _moe_ffn(texp, nvalid, xs, wg, wu, wd, tm):
    n_tiles = xs.shape[0] // tm
    wspec = lambda s: pl.BlockSpec((1,) + s, lambda i, te, nv: (te[i], 0, 0))
    row = pl.BlockSpec((tm, D_MODEL), lambda i, te, nv: (i, 0))
    return pl.pallas_call(
        _moe_ffn_kernel,
        grid_spec=pltpu.PrefetchScalarGridSpec(
            num_scalar_prefetch=2,
            grid=(n_tiles,),
            in_specs=[row, wspec((D_MODEL, D_FF)), wspec((D_MODEL, D_FF)), wspec((D_FF, D_MODEL))],
            out_specs=row,
            scratch_shapes=[pltpu.VMEM((tm, D_FF), BF16)],
        ),
        out_shape=jax.ShapeDtypeStruct(xs.shape, F32),
        compiler_params=_params("arbitrary"),
        name="moe_ffn",
    )(texp, nvalid, xs, wg, wu, wd)


def _moe_combine_kernel(pos_ref, next_ref, x_ref, g_ref, ys_hbm, o_ref, ybuf, sem, *, tt, nsteps):
    i = pl.program_id(0)

    def fetch(table_ref, s):
        _start_row_copies(table_ref, tt, lambda k, j, src: pltpu.make_async_copy(
            ys_hbm.at[pl.ds(src, 1), :], ybuf.at[s, k, pl.ds(j, 1), :], sem.at[s]))

    @pl.when(i == 0)
    def _():
        fetch(pos_ref, 0)

    for s in range(2):
        @pl.when((i % 2 == s) & (i + 1 < nsteps))
        def _():
            fetch(next_ref, 1 - s)

        @pl.when(i % 2 == s)
        def _():
            for k in range(2):
                pltpu.make_async_copy(ys_hbm.at[pl.ds(0, tt), :], ybuf.at[s, k], sem.at[s]).wait()
            g = g_ref[...]
            o_ref[...] = x_ref[...] + (g[:, 0:1] * ybuf[s, 0] + g[:, 1:2] * ybuf[s, 1])


def _moe_combine(pos_tab, x, gates, ys):
    t = x.shape[0]
    nsteps, _, tt2 = pos_tab.shape
    tt = tt2 // 2
    kern = functools.partial(_moe_combine_kernel, tt=tt, nsteps=nsteps)
    row = lambda n: pl.BlockSpec((tt, n), lambda i: (i, 0))
    table = lambda off: pl.BlockSpec((1, 1, tt2), lambda i: (jnp.minimum(i + off, nsteps - 1), 0, 0),
                                     memory_space=pltpu.SMEM)
    return pl.pallas_call(
        kern,
        grid=(nsteps,),
        in_specs=[table(0), table(1), row(D_MODEL), row(2), pl.BlockSpec(memory_space=pl.ANY)],
        out_specs=row(D_MODEL),
        out_shape=jax.ShapeDtypeStruct((t, D_MODEL), F32),
        scratch_shapes=[pltpu.VMEM((2, 2, tt, D_MODEL), F32), pltpu.SemaphoreType.DMA((2,))],
        compiler_params=_params("arbitrary"),
        name="moe_combine",
    )(pos_tab, pos_tab, x, gates, ys)


def _moe(x, gain, router, wg, wu, wd):
    t = x.shape[0]
    tm = TOKEN_TILE if 2 * t >= N_EXPERTS * 4 * TOKEN_TILE else TOKEN_TILE // 2
    comb, mask = _moe_route(x, gain, router)
    texp, nvalid, last_tile, pos_tab, gates, n_tiles = _moe_plan(mask, comb, tm)
    xs = _moe_dispatch(last_tile, pos_tab, x, gain, n_tiles * tm, tm)
    ys = _moe_ffn(texp, nvalid, xs, wg, wu, wd, tm)
    return _moe_combine(pos_tab, x, gates, ys)


def _pair_states(s):
    b = s.shape[0]
    s = s.reshape(b, RWKV_HEADS // 2, 2, RWKV_HEAD, RWKV_HEAD)
    z = jnp.zeros_like(s[:, :, 0])
    top = jnp.concatenate([s[:, :, 0], z], axis=-1)
    bot = jnp.concatenate([z, s[:, :, 1]], axis=-1)
    return jnp.concatenate([top, bot], axis=-2)


def _unpair_states(s):
    b = s.shape[0]
    s0 = s[:, :, :RWKV_HEAD, :RWKV_HEAD]
    s1 = s[:, :, RWKV_HEAD:, RWKV_HEAD:]
    return jnp.stack([s0, s1], axis=2).reshape(b, RWKV_HEADS, RWKV_HEAD, RWKV_HEAD)


def _trunk(x, ssm, conv, ck, cv, wkv, shift, w):
    b, l, _ = x.shape
    t = b * l
    ones_blk = w["ones_blk"]
    xf = x.reshape(t, D_MODEL)

    z, xbc, dt, q, k, v = _inproj(xf, w["norm_mix0"], w["wz"], w["wxbc"], w["wdt"], w["wq"], w["wk"], w["wv"],
                                  w["q_norm"], w["k_norm"], ones_blk)
    lp = l if l % SSD_CHUNK == 0 else 8
    seq = lambda a: a.reshape(b, l, a.shape[-1])
    padl = lambda a: a if lp == l else jnp.pad(a, ((0, 0), (0, lp - l), (0, 0)))
    y_ssd, ssm1, conv1 = _ssd(padl(seq(xbc)), padl(seq(dt)), padl(seq(z)), conv, ssm.reshape(b, SSD_WIDTH, SSD_STATE),
                              w["conv_w"], w["conv_b"], w["dt_bias"], w["a_log"], w["d_skip"], w["ssd_norm"], lv=l)
    y_ssd = y_ssd[:, :l].reshape(t, SSD_WIDTH)
    ssm1 = ssm1.reshape(b, SSD_HEADS, SSD_HEAD_DIM, SSD_STATE)
    if ck is None:
        o = _swa_prompt(seq(q), seq(k), seq(v), w["sinks"]).reshape(t, Q_DIM)
        k1 = seq(k)[:, -WINDOW:].reshape(b, WINDOW, ATT_KV_HEADS, ATT_HEAD_DIM)
        v1 = seq(v)[:, -WINDOW:].reshape(b, WINDOW, ATT_KV_HEADS, ATT_HEAD_DIM)
    else:
        wc = ck.shape[1]
        o, k1, v1 = _swa_cached(q, k, v, ck.reshape(b, wc, KV_DIM), cv.reshape(b, wc, KV_DIM), w["sinks"], l=l)
        k1 = k1.reshape(b, wc, ATT_KV_HEADS, ATT_HEAD_DIM)
        v1 = v1.reshape(b, wc, ATT_KV_HEADS, ATT_HEAD_DIM)
    x2 = _outproj_ffn(xf, y_ssd, o, w["wo_a"], w["wo_b"], w["norm_ffn0"], w["ffn_gate"], w["ffn_up"], w["ffn_down"])

    tm = min(TOKEN_TILE, t)
    if l >= tm:
        first = shift.reshape(b, 1, D_MODEL)
    else:
        first = jnp.repeat(shift, l, axis=0)
    r, lw, kx, vx, an, bb, gate, h1 = _rwkv_prep(
        x2, first, w["norm_mix1"], w["mu"], w["w0"], w["a0"], w["kk"], w["ka"], w["wr"], w["wkk"], w["wvv"],
        w["w1"], w["w2"], w["a1"], w["a2"], w["g1"], w["g2"], ones_blk, seq_len=l)
    shift1 = h1.reshape(b, D_MODEL) if l >= tm else h1.reshape(b, l, D_MODEL)[:, -1]
    if l >= RWKV_CHUNK:
        lpr = -(-l // RWKV_CHUNK) * RWKV_CHUNK
    else:
        lpr = max(RWKV_SHORT_CHUNK, pl.next_power_of_2(l))
    padr = lambda a: seq(a) if lpr == l else jnp.pad(seq(a), ((0, 0), (0, lpr - l), (0, 0)))
    y, s1 = _rwkv_scan(padr(r), padr(lw), padr(kx), padr(vx), padr(an), padr(bb), _pair_states(wkv))
    y = y[:, :l].reshape(t, D_MODEL)
    x3 = _rwkv_out(x2, y, r, kx, vx, gate, w["ln_w"], w["ln_b"], w["rk"], w["wo"], ones_blk)
    x4 = _moe(x3, w["norm_ffn1"], w["router"], w["moe_gate"], w["moe_up"], w["moe_down"])
    return (x4.reshape(b, l, D_MODEL), ssm1[None], conv1[None], k1[None], v1[None],
            _unpair_states(s1)[None], shift1[None])


def kernel(x_prompt, x_sample, state_ssm, state_conv, cache_swa_k, cache_swa_v, state_wkv, state_shift, norm_mix, norm_ffn, w_in, conv_w, conv_b, dt_bias, a_log, d_skip, ssd_norm, q_norm, k_norm, attn_sinks, w_out, ffn_gate, ffn_up, ffn_down, rwkv_mu, rwkv_w0, rwkv_w1, rwkv_w2, rwkv_a0, rwkv_a1, rwkv_a2, rwkv_g1, rwkv_g2, rwkv_kk, rwkv_ka, rwkv_rk, rwkv_wr, rwkv_wk, rwkv_wv, rwkv_wo, rwkv_ln_w, rwkv_ln_b, moe_router, moe_gate, moe_up, moe_down):
    bf = lambda a: a.astype(BF16)
    row = lambda a: a.reshape(1, -1).astype(F32)
    padlane = lambda a: jnp.pad(a, ((0, 0), (0, LANES - a.shape[1])))
    wi = w_in[0]
    c0 = SSD_WIDTH
    c1 = c0 + CONV_DIM
    c2 = c1 + SSD_HEADS
    c3 = c2 + Q_DIM
    c4 = c3 + KV_DIM
    w = dict(
        ones_blk=_block_ones(),
        norm_mix0=row(norm_mix[0]), norm_mix1=row(norm_mix[1]), norm_ffn0=row(norm_ffn[0]), norm_ffn1=row(norm_ffn[1]),
        wz=bf(wi[:, :c0]), wxbc=bf(wi[:, c0:c1]), wdt=bf(padlane(wi[:, c1:c2])), wq=bf(wi[:, c2:c3]),
        wk=bf(wi[:, c3:c4]), wv=bf(wi[:, c4:]),
        q_norm=row(jnp.tile(q_norm[0], ATT_HEADS)), k_norm=row(jnp.tile(k_norm[0], ATT_KV_HEADS)),
        conv_w=conv_w[0], conv_b=row(conv_b[0]), dt_bias=padlane(row(dt_bias[0])), a_log=padlane(row(a_log[0])),
        d_skip=row(jnp.repeat(d_skip[0], SSD_HEAD_DIM)), ssd_norm=row(ssd_norm[0]), sinks=attn_sinks[0].astype(F32),
        wo_a=bf(w_out[0, :SSD_WIDTH]), wo_b=bf(w_out[0, SSD_WIDTH:]),
        ffn_gate=bf(ffn_gate[0]), ffn_up=bf(ffn_up[0]), ffn_down=bf(ffn_down[0]),
        mu=rwkv_mu[0], w0=row(rwkv_w0[0]), a0=row(rwkv_a0[0]), kk=row(rwkv_kk[0]), ka=row(rwkv_ka[0]),
        wr=bf(rwkv_wr[0]), wkk=bf(rwkv_wk[0]), wvv=bf(rwkv_wv[0]), wo=bf(rwkv_wo[0]),
        w1=bf(rwkv_w1[0]), w2=bf(rwkv_w2[0]), a1=bf(rwkv_a1[0]), a2=bf(rwkv_a2[0]),
        g1=bf(rwkv_g1[0]), g2=bf(rwkv_g2[0]),
        ln_w=row(rwkv_ln_w[0]), ln_b=row(rwkv_ln_b[0]), rk=row(rwkv_rk[0]),
        router=padlane(moe_router[0]), moe_gate=bf(moe_gate[0]), moe_up=bf(moe_up[0]), moe_down=bf(moe_down[0]),
    )
    bp = x_prompt.shape[0]
    z_ssm = jnp.zeros((bp,) + state_ssm.shape[2:], F32)
    z_conv = jnp.zeros((bp,) + state_conv.shape[2:], F32)
    z_wkv = jnp.zeros((bp,) + state_wkv.shape[2:], F32)
    z_shift = jnp.zeros((bp,) + state_shift.shape[2:], F32)
    outs_p = _trunk(x_prompt, z_ssm, z_conv, None, None, z_wkv, z_shift, w)
    outs_s = _trunk(x_sample, state_ssm[0], state_conv[0], cache_swa_k[0], cache_swa_v[0], state_wkv[0],
                    state_shift[0], w)
    return (outs_p[0], outs_s[0]) + tuple(outs_p[1:]) + tuple(outs_s[1:])
```

```python
import functools

import jax
import jax.numpy as jnp
from jax import lax
from jax.experimental import pallas as pl
from jax.experimental.pallas import tpu as pltpu

F32 = jnp.float32
BF16 = jnp.bfloat16

D_MODEL = 1024
SSD_HEAD_DIM = 64
SSD_HEADS = 16
SSD_GROUPS = 2
SSD_STATE = 128
SSD_CONV = 4
SSD_CHUNK = 128
SSD_SHORT_CHUNK = 16
SSD_WIDTH = 1024
CONV_DIM = SSD_WIDTH + 2 * SSD_GROUPS * SSD_STATE
ATT_HEAD_DIM = 64
ATT_HEADS = 16
ATT_KV_HEADS = 4
ATT_REP = ATT_HEADS // ATT_KV_HEADS
WINDOW = 128
Q_DIM = ATT_HEADS * ATT_HEAD_DIM
KV_DIM = ATT_KV_HEADS * ATT_HEAD_DIM
RWKV_HEAD = 64
RWKV_HEADS = 16
RWKV_GN_EPS = 64e-5
RWKV_CHUNK = 64
RWKV_SHORT_CHUNK = 16
D_FF = 2816
N_EXPERTS = 8
RMS_EPS = 1e-6
NEG_INF = -1e30

LANES = 128
MXU_DIM = 256
VMEM_LIMIT_BYTES = 56 * 1024 * 1024
TOKEN_TILE = 512
FF_CHUNK = MXU_DIM
HIGHEST = lax.Precision.HIGHEST


def _dot(a, b, precision=None):
    return jnp.dot(a, b, preferred_element_type=F32, precision=precision)


def _dot_nt(a, b, precision=None):
    return lax.dot_general(a, b, (((1,), (1,)), ((), ())), preferred_element_type=F32, precision=precision)


def _dot_tn(a, b, precision=None):
    return lax.dot_general(a, b, (((0,), (0,)), ((), ())), preferred_element_type=F32, precision=precision)


def _rms_rows(x, g):
    return x * lax.rsqrt(jnp.mean(x * x, axis=-1, keepdims=True) + RMS_EPS) * g


def _sigmoid(x):
    return 1.0 / (1.0 + jnp.exp(-x))


def _silu(x):
    return x * _sigmoid(x)


def _softplus(x):
    return jnp.maximum(x, 0.0) + jnp.log(1.0 + jnp.exp(-jnp.abs(x)))


def _const_spec(shape):
    zeros = (0,) * len(shape)
    return pl.BlockSpec(shape, lambda *_: zeros, pipeline_mode=pl.Buffered(1))


def _params(*semantics):
    return pltpu.CompilerParams(dimension_semantics=semantics, vmem_limit_bytes=VMEM_LIMIT_BYTES)


def _head_sums(t, ones_blk):
    return _dot(t.astype(BF16), ones_blk)


def _block_ones():
    r = jnp.arange(MXU_DIM) // ATT_HEAD_DIM
    return (r[:, None] == r[None, :]).astype(BF16)


def _inproj_kernel(x_ref, g_ref, wz_ref, wxbc_ref, wdt_ref, wq_ref, wk_ref, wv_ref, qn_ref, kn_ref, ones_ref,
                   z_ref, xbc_ref, dt_ref, q_ref, k_ref, v_ref):
    h = _rms_rows(x_ref[...], g_ref[...]).astype(BF16)
    ones_blk = ones_ref[...]

    def head_rms(t, gain):
        outs = []
        for c in range(t.shape[1] // MXU_DIM):
            tc = t[:, c * MXU_DIM:(c + 1) * MXU_DIM]
            ms = _head_sums(tc * tc, ones_blk) * (1.0 / ATT_HEAD_DIM)
            outs.append(tc * lax.rsqrt(ms + RMS_EPS))
        return jnp.concatenate(outs, axis=1) * gain

    z_ref[...] = _dot(h, wz_ref[...]).astype(z_ref.dtype)
    xbc_ref[...] = _dot(h, wxbc_ref[...])
    dt_ref[...] = _dot(h, wdt_ref[...])
    q_ref[...] = head_rms(_dot(h, wq_ref[...]), qn_ref[...]).astype(q_ref.dtype)
    k_ref[...] = head_rms(_dot(h, wk_ref[...]), kn_ref[...])
    v_ref[...] = _dot(h, wv_ref[...])


def _inproj(x, gain, wz, wxbc, wdt, wq, wk, wv, qn, kn, ones_blk):
    t = x.shape[0]
    tm = min(TOKEN_TILE, t)
    row = lambda n: pl.BlockSpec((tm, n), lambda i: (i, 0))
    return pl.pallas_call(
        _inproj_kernel,
        grid=(t // tm,),
        in_specs=[row(D_MODEL), _const_spec((1, D_MODEL)),
                  _const_spec(wz.shape), _const_spec(wxbc.shape), _const_spec(wdt.shape),
                  _const_spec(wq.shape), _const_spec(wk.shape), _const_spec(wv.shape),
                  _const_spec((1, Q_DIM)), _const_spec((1, KV_DIM)), _const_spec((MXU_DIM, MXU_DIM))],
        out_specs=[row(SSD_WIDTH), row(CONV_DIM), row(LANES), row(Q_DIM), row(KV_DIM), row(KV_DIM)],
        out_shape=[jax.ShapeDtypeStruct((t, SSD_WIDTH), BF16), jax.ShapeDtypeStruct((t, CONV_DIM), F32),
                   jax.ShapeDtypeStruct((t, LANES), F32), jax.ShapeDtypeStruct((t, Q_DIM), BF16),
                   jax.ShapeDtypeStruct((t, KV_DIM), F32), jax.ShapeDtypeStruct((t, KV_DIM), F32)],
        compiler_params=_params("arbitrary"),
        name="inproj",
    )(x, gain, wz, wxbc, wdt, wq, wk, wv, qn, kn, ones_blk)


def _ssd_kernel(xbc_ref, dt_ref, z_ref, conv0_ref, ssm0_ref, cw_ref, cb_ref, dtb_ref, alog_ref, dsk_ref, nrm_ref,
                y_ref, ssm1_ref, conv1_ref, xpad_ref, dtpad_ref, s_ref, *, q, lin, lv, nchunks):
    c = pl.program_id(1)
    n_pairs = SSD_HEADS // 2
    hist = SSD_CONV - 1
    base = 8 - hist

    @pl.when(c == 0)
    def _():
        xpad_ref[...] = jnp.zeros(xpad_ref.shape, F32)
        dtpad_ref[...] = jnp.zeros(dtpad_ref.shape, F32)
        xpad_ref[base:8, :] = conv0_ref[0]
        for jb in range(n_pairs):
            s_ref[:, jb * LANES:(jb + 1) * LANES] = ssm0_ref[0, jb * LANES:(jb + 1) * LANES, :].T

    xpad_ref[8:8 + lin, :] = xbc_ref[0]
    dtpad_ref[0:lin, :] = dt_ref[0]

    conv = cb_ref[...]
    for j in range(SSD_CONV):
        conv = conv + xpad_ref[base + j:base + j + q, :] * cw_ref[j:j + 1, :]
    tail = xpad_ref[base + lv:8 + lv, :]
    conv1_ref[0] = tail
    xpad_ref[base:8, :] = tail

    xc = _silu(conv)
    xs = xc[:, :SSD_WIDTH]
    bm = xc[:, SSD_WIDTH:SSD_WIDTH + SSD_GROUPS * SSD_STATE].astype(BF16)
    cm = xc[:, SSD_WIDTH + SSD_GROUPS * SSD_STATE:].astype(BF16)

    row_i = lax.broadcasted_iota(jnp.int32, (q, LANES), 0)
    col_i = lax.broadcasted_iota(jnp.int32, (q, LANES), 1)
    first_half = col_i < SSD_HEAD_DIM
    dt = jnp.where(row_i < lv, _softplus(dtpad_ref[...] + dtb_ref[...]), 0.0)
    a = dt * (-jnp.exp(alog_ref[...]))
    r_q = lax.broadcasted_iota(jnp.int32, (q, q), 0)
    c_q = lax.broadcasted_iota(jnp.int32, (q, q), 1)
    causal = r_q >= c_q
    a_cs = _dot(causal.astype(F32), a, precision=HIGHEST)
    if q == LANES:
        a_cs_t = a_cs.T
    else:
        a_cs_t = jnp.concatenate([a_cs, jnp.zeros((LANES - q, LANES), F32)], axis=0).T[:, :q]
    ea = jnp.exp(a_cs)
    dte = jnp.exp(a_cs[q - 1:q, :] - a_cs)

    def colb(arr, h):
        return jnp.broadcast_to(arr[:, h:h + 1], (q, LANES))

    def pairb(arr, h0):
        return jnp.where(first_half, colb(arr, h0), colb(arr, h0 + 1))

    cbs = []
    for g in range(SSD_GROUPS):
        sl = slice(g * SSD_STATE, (g + 1) * SSD_STATE)
        cbs.append(_dot_nt(cm[:, sl], bm[:, sl]))

    ys = []
    for j in range(n_pairs):
        g = (2 * j) // (SSD_HEADS // SSD_GROUPS)
        gsl = slice(g * SSD_STATE, (g + 1) * SSD_STATE)
        psl = slice(j * LANES, (j + 1) * LANES)
        xs_p = xs[:, psl]
        xdt = xs_p * pairb(dt, 2 * j)
        xdt_b = xdt.astype(BF16)
        yd = []
        for hh in (2 * j, 2 * j + 1):
            diff = colb(a_cs, hh)[:, :q] - a_cs_t[hh:hh + 1, :]
            lmat = jnp.exp(jnp.where(causal, diff, NEG_INF))
            yd.append(_dot((cbs[g] * lmat).astype(BF16), xdt_b))
        y_diag = jnp.where(first_half, yd[0], yd[1])
        ea_p = pairb(ea, 2 * j)
        s_old = s_ref[:, psl]
        y_off = _dot(cm[:, gsl], s_old.astype(BF16)) * ea_p
        s_ref[:, psl] = s_old * ea_p[q - 1:q, :] + _dot_tn(bm[:, gsl], (xdt * pairb(dte, 2 * j)).astype(BF16))
        ys.append(y_diag + y_off + xs_p * dsk_ref[:, psl])
    y = jnp.concatenate(ys, axis=1)[0:lin]

    y = y * _silu(z_ref[0].astype(F32))
    gw = SSD_WIDTH // SSD_GROUPS
    outs = []
    for g in range(SSD_GROUPS):
        yg = y[:, g * gw:(g + 1) * gw]
        outs.append(yg * lax.rsqrt(jnp.mean(yg * yg, axis=-1, keepdims=True) + RMS_EPS))
    y_ref[0] = (jnp.concatenate(outs, axis=1) * nrm_ref[...]).astype(y_ref.dtype)

    @pl.when(c == nchunks - 1)
    def _():
        for jb in range(n_pairs):
            ssm1_ref[0, jb * LANES:(jb + 1) * LANES, :] = s_ref[:, jb * LANES:(jb + 1) * LANES].T


def _ssd(xbc, dt, z, conv0, ssm0, cw, cb, dtb, alog, dsk, nrm, *, lv):
    b, lp, _ = xbc.shape
    q = SSD_CHUNK if lp >= SSD_CHUNK else SSD_SHORT_CHUNK
    lin = min(q, lp)
    nchunks = lp // lin
    lv_chunk = min(lv, lin)
    kern = functools.partial(_ssd_kernel, q=q, lin=lin, lv=lv_chunk, nchunks=nchunks)
    seq = lambda n: pl.BlockSpec((1, lin, n), lambda i, c: (i, c, 0))
    per_b = lambda s: pl.BlockSpec((1,) + s, lambda i, c: (i, 0, 0))
    return pl.pallas_call(
        kern,
        grid=(b, nchunks),
        in_specs=[seq(CONV_DIM), seq(LANES), seq(SSD_WIDTH),
                  per_b((SSD_CONV - 1, CONV_DIM)), per_b((SSD_WIDTH, SSD_STATE)),
                  _const_spec((SSD_CONV, CONV_DIM)), _const_spec((1, CONV_DIM)), _const_spec((1, LANES)),
                  _const_spec((1, LANES)), _const_spec((1, SSD_WIDTH)), _const_spec((1, SSD_WIDTH))],
        out_specs=[seq(SSD_WIDTH), per_b((SSD_WIDTH, SSD_STATE)), per_b((SSD_CONV - 1, CONV_DIM))],
        out_shape=[jax.ShapeDtypeStruct((b, lp, SSD_WIDTH), BF16),
                   jax.ShapeDtypeStruct((b, SSD_WIDTH, SSD_STATE), F32),
                   jax.ShapeDtypeStruct((b, SSD_CONV - 1, CONV_DIM), F32)],
        scratch_shapes=[pltpu.VMEM((q + 8, CONV_DIM), F32), pltpu.VMEM((q, LANES), F32),
                        pltpu.VMEM((SSD_STATE, SSD_WIDTH), F32)],
        compiler_params=_params("arbitrary", "arbitrary"),
        name="ssd",
    )(xbc, dt, z, conv0, ssm0, cw, cb, dtb, alog, dsk, nrm)


def _sink_softmax(s, allowed, sink_col):
    s = jnp.where(allowed, s, NEG_INF)
    m = jnp.maximum(jnp.max(s, axis=-1, keepdims=True), sink_col)
    return jnp.exp(s - m).astype(BF16), jnp.exp(sink_col - m)


def _gqa_blocks(blocks, sink_ref, allowed, nq, group):
    d = ATT_HEAD_DIM
    sinks = [jnp.concatenate([jnp.full((nq, 1), sink_ref[ATT_REP * j + r], F32) for r in range(ATT_REP)], axis=0)
             for j in range(ATT_KV_HEADS)]
    problems = [(bi, j) for bi in range(len(blocks)) for j in range(ATT_KV_HEADS)]
    ones_k = jnp.ones((blocks[0][1].shape[0], d), BF16)
    outs = {}
    for g0 in range(0, len(problems), group):
        grp = problems[g0:g0 + group]
        scores = []
        for bi, j in grp:
            qb, kk, _ = blocks[bi]
            qj = jnp.concatenate(
                [qb[:, (ATT_REP * j + r) * d:(ATT_REP * j + r + 1) * d] for r in range(ATT_REP)], axis=0)
            scores.append(_dot_nt(qj * (d ** -0.5), kk[:, j * d:(j + 1) * d]))
        probs = [_sink_softmax(s, allowed, sinks[j]) for s, (_, j) in zip(scores, grp)]
        for (p, sink_term), (bi, j) in zip(probs, grp):
            denom = _dot(p, ones_k) + sink_term
            outs[bi, j] = _dot(p, blocks[bi][2][:, j * d:(j + 1) * d]) / denom
    return [jnp.concatenate([outs[bi, j][r * nq:(r + 1) * nq] for j in range(ATT_KV_HEADS) for r in range(ATT_REP)],
                            axis=1) for bi in range(len(blocks))]


def _swa_prompt_kernel(sink_ref, q_ref, kp_ref, kc_ref, vp_ref, vc_ref, o_ref):
    i = pl.program_id(1)
    w = WINDOW
    kk = jnp.concatenate([kp_ref[0], kc_ref[0]], axis=0).astype(BF16)
    vv = jnp.concatenate([vp_ref[0], vc_ref[0]], axis=0).astype(BF16)
    rows = lax.broadcasted_iota(jnp.int32, (ATT_REP * w, 2 * w), 0) % w
    cols = lax.broadcasted_iota(jnp.int32, (ATT_REP * w, 2 * w), 1)
    rel = rows + w - cols
    allowed = (rel >= 0) & (rel <= w) & ((cols >= w) | (i > 0))
    o_ref[0] = _gqa_blocks([(q_ref[0], kk, vv)], sink_ref, allowed, w, group=ATT_KV_HEADS)[0].astype(o_ref.dtype)


def _swa_prompt(q, k, v, sinks):
    b, l, _ = q.shape
    nb = l // WINDOW
    cur = lambda n: pl.BlockSpec((1, WINDOW, n), lambda i, c: (i, c, 0))
    prev = lambda n: pl.BlockSpec((1, WINDOW, n), lambda i, c: (i, jnp.maximum(c - 1, 0), 0))
    return pl.pallas_call(
        _swa_prompt_kernel,
        grid=(b, nb),
        in_specs=[pl.BlockSpec(memory_space=pltpu.SMEM),
                  cur(Q_DIM), prev(KV_DIM), cur(KV_DIM), prev(KV_DIM), cur(KV_DIM)],
        out_specs=cur(Q_DIM),
        out_shape=jax.ShapeDtypeStruct((b, l, Q_DIM), BF16),
        compiler_params=_params("arbitrary", "arbitrary"),
        name="swa_prompt",
    )(sinks, q, k, k, v, v)


def _swa_cached_kernel(sink_ref, q_ref, k_ref, v_ref, kbuf_ref, vbuf_ref, o_ref, k1_ref, v1_ref, *, nbatch, l):
    wc = kbuf_ref.shape[1]
    pad = 8 - l
    nk = wc + 8
    rows = lax.broadcasted_iota(jnp.int32, (ATT_REP * l, nk), 0) % l
    cols = lax.broadcasted_iota(jnp.int32, (ATT_REP * l, nk), 1)
    rel = rows + wc - cols
    allowed = (rel >= 0) & (rel <= WINDOW)
    zpad = jnp.zeros((pad, KV_DIM), F32)
    qall = q_ref[...].astype(F32)
    blocks = []
    for bi in range(nbatch):
        kn = k_ref[bi * l:(bi + 1) * l, :]
        vn = v_ref[bi * l:(bi + 1) * l, :]
        kc = jnp.concatenate([kbuf_ref[bi], kn, zpad], axis=0).astype(BF16)
        vc = jnp.concatenate([vbuf_ref[bi], vn, zpad], axis=0).astype(BF16)
        blocks.append((qall[bi * l:(bi + 1) * l, :].astype(BF16), kc, vc))
        k1_ref[bi, 0:wc - l, :] = kbuf_ref[bi, l:wc, :]
        k1_ref[bi, wc - l:wc, :] = kn
        v1_ref[bi, 0:wc - l, :] = vbuf_ref[bi, l:wc, :]
        v1_ref[bi, wc - l:wc, :] = vn
    o_ref[...] = jnp.concatenate(_gqa_blocks(blocks, sink_ref, allowed, l, group=ATT_KV_HEADS * nbatch), axis=0).astype(o_ref.dtype)


def _swa_cached(q, k, v, kbuf, vbuf, sinks, *, l):
    b, wc, _ = kbuf.shape
    nbatch = 8
    kern = functools.partial(_swa_cached_kernel, nbatch=nbatch, l=l)
    rows = lambda n: pl.BlockSpec((nbatch * l, n), lambda i: (i, 0))
    cache = pl.BlockSpec((nbatch, wc, KV_DIM), lambda i: (i, 0, 0))
    return pl.pallas_call(
        kern,
        grid=(b // nbatch,),
        in_specs=[pl.BlockSpec(memory_space=pltpu.SMEM), rows(Q_DIM), rows(KV_DIM), rows(KV_DIM), cache, cache],
        out_specs=[rows(Q_DIM), cache, cache],
        out_shape=[jax.ShapeDtypeStruct((b * l, Q_DIM), BF16),
                   jax.ShapeDtypeStruct((b, wc, KV_DIM), F32), jax.ShapeDtypeStruct((b, wc, KV_DIM), F32)],
        compiler_params=_params("arbitrary"),
        name="swa_cached",
    )(sinks, q, k, v, kbuf, vbuf)


def _swiglu_act(h, wg_ref, wu_ref, act_ref):
    for c in range(D_FF // FF_CHUNK):
        sl = slice(c * FF_CHUNK, (c + 1) * FF_CHUNK)
        act_ref[:, sl] = (_silu(_dot(h, wg_ref[:, sl])) * _dot(h, wu_ref[:, sl])).astype(BF16)


def _outproj_ffn_kernel(x_ref, ya_ref, yb_ref, woa_ref, wob_ref, g_ref, wg_ref, wu_ref, wd_ref, o_ref, act_ref):
    x1 = x_ref[...] + _dot(ya_ref[...], woa_ref[...]) + _dot(yb_ref[...], wob_ref[...])
    h = _rms_rows(x1, g_ref[...]).astype(BF16)
    _swiglu_act(h, wg_ref, wu_ref, act_ref)
    o_ref[...] = x1 + _dot(act_ref[...], wd_ref[...])


def _outproj_ffn(x, ya, yb, woa, wob, gain, wg, wu, wd):
    t = x.shape[0]
    tm = min(TOKEN_TILE, t)
    row = lambda n: pl.BlockSpec((tm, n), lambda i: (i, 0))
    return pl.pallas_call(
        _outproj_ffn_kernel,
        grid=(t // tm,),
        in_specs=[row(D_MODEL), row(SSD_WIDTH), row(Q_DIM), _const_spec(woa.shape), _const_spec(wob.shape),
                  _const_spec((1, D_MODEL)), _const_spec(wg.shape), _const_spec(wu.shape), _const_spec(wd.shape)],
        out_specs=row(D_MODEL),
        out_shape=jax.ShapeDtypeStruct((t, D_MODEL), F32),
        scratch_shapes=[pltpu.VMEM((tm, D_FF), BF16)],
        compiler_params=_params("arbitrary"),
        name="outproj_ffn",
    )(x, ya, yb, woa, wob, gain, wg, wu, wd)


def _rwkv_prep_kernel(x_ref, first_ref, g_ref, mu_ref, w0_ref, a0_ref, kkc_ref, ka_ref,
                      wr_ref, wk_ref, wv_ref, w1_ref, w2_ref, a1_ref, a2_ref, g1_ref, g2_ref, ones_ref,
                      r_ref, lw_ref, k_ref, v_ref, an_ref, b_ref, gate_ref, h_ref, hs_ref, *, tm, seq_len):
    i = pl.program_id(0)
    h = _rms_rows(x_ref[...], g_ref[...])
    tiles_per_seq = max(seq_len // tm, 1)

    if seq_len >= tm:
        h_ref[0] = h[tm - 1:tm, :]

        @pl.when(i % tiles_per_seq == 0)
        def _():
            hs_ref[7:8, :] = first_ref[0]
    else:
        h_ref[...] = h
        hs_ref[0:8, :] = jnp.zeros((8, D_MODEL), F32)
    hs_ref[8:8 + tm, :] = h
    prev = hs_ref[7:7 + tm, :]
    if seq_len < tm:
        rows = lax.broadcasted_iota(jnp.int32, (tm, D_MODEL), 0)
        prev = jnp.where(rows % seq_len == 0, first_ref[...], prev)
    else:
        hs_ref[7:8, :] = h[tm - 1:tm, :]
    xx = prev - h
    mix = lambda n: (h + xx * mu_ref[n:n + 1, :]).astype(BF16)
    r = _dot(mix(0), wr_ref[...])
    k = _dot(mix(2), wk_ref[...])
    v = _dot(mix(3), wv_ref[...])
    w_lora = _dot(jnp.tanh(_dot(mix(1), w1_ref[...])).astype(BF16), w2_ref[...])
    a_lora = _dot(_dot(mix(4), a1_ref[...]).astype(BF16), a2_ref[...])
    gate = _dot(_sigmoid(_dot(mix(5), g1_ref[...])).astype(BF16), g2_ref[...])
    w_log = -_softplus(-(w0_ref[...] + w_lora)) - 0.5
    a_sig = _sigmoid(a0_ref[...] + a_lora)
    kk = k * kkc_ref[...]
    ones_blk = ones_ref[...]
    parts = []
    for c in range(D_MODEL // MXU_DIM):
        kc = kk[:, c * MXU_DIM:(c + 1) * MXU_DIM]
        nrm = jnp.maximum(jnp.sqrt(_head_sums(kc * kc, ones_blk)), 1e-12)
        parts.append(kc / nrm)
    kk = jnp.concatenate(parts, axis=1)
    r_ref[...] = r.astype(r_ref.dtype)
    lw_ref[...] = -jnp.exp(w_log)
    k_ref[...] = (k * (1.0 + (a_sig - 1.0) * ka_ref[...])).astype(k_ref.dtype)
    v_ref[...] = v.astype(v_ref.dtype)
    an_ref[...] = (-kk).astype(an_ref.dtype)
    b_ref[...] = (kk * a_sig).astype(b_ref.dtype)
    gate_ref[...] = gate.astype(gate_ref.dtype)


def _rwkv_prep(x, first, gain, mu, w0, a0, kkc, ka, wr, wk, wv, w1, w2, a1, a2, g1, g2, ones_blk, *, seq_len):
    t = x.shape[0]
    tm = min(TOKEN_TILE, t)
    kern = functools.partial(_rwkv_prep_kernel, tm=tm, seq_len=seq_len)
    row = lambda n: pl.BlockSpec((tm, n), lambda i: (i, 0))
    if seq_len >= tm:
        tiles_per_seq = seq_len // tm
        first_spec = pl.BlockSpec((1, 1, D_MODEL), lambda i: (i // tiles_per_seq, 0, 0))
        last_spec = first_spec
        last_shape = jax.ShapeDtypeStruct((t // seq_len, 1, D_MODEL), F32)
    else:
        first_spec = row(D_MODEL)
        last_spec = row(D_MODEL)
        last_shape = jax.ShapeDtypeStruct((t, D_MODEL), F32)
    vec = _const_spec((1, D_MODEL))
    wide = lambda dt: jax.ShapeDtypeStruct((t, D_MODEL), dt)
    outs = [wide(BF16), wide(F32), wide(BF16), wide(BF16), wide(BF16), wide(BF16), wide(BF16), last_shape]
    return pl.pallas_call(
        kern,
        grid=(t // tm,),
        in_specs=[row(D_MODEL), first_spec, vec, _const_spec((6, D_MODEL)), vec, vec, vec, vec,
                  _const_spec(wr.shape), _const_spec(wk.shape), _const_spec(wv.shape),
                  _const_spec(w1.shape), _const_spec(w2.shape), _const_spec(a1.shape), _const_spec(a2.shape),
                  _const_spec(g1.shape), _const_spec(g2.shape), _const_spec((MXU_DIM, MXU_DIM))],
        out_specs=[row(D_MODEL)] * 7 + [last_spec],
        out_shape=outs,
        scratch_shapes=[pltpu.VMEM((tm + 8, D_MODEL), F32)],
        compiler_params=_params("arbitrary"),
        name="rwkv_prep",
    )(x, first, gain, mu, w0, a0, kkc, ka, wr, wk, wv, w1, w2, a1, a2, g1, g2, ones_blk)


def _rwkv_scan_kernel(r_ref, lw_ref, k_ref, v_ref, a_ref, b_ref, s0_ref, y_ref, s1_ref, s_ref, *, c, nsub, nsteps):
    step = pl.program_id(1)
    n_pairs = RWKV_HEADS // 2
    two_c = 2 * c
    levels = c.bit_length() - 1

    @pl.when(step == 0)
    def _():
        s_ref[...] = s0_ref[0]

    lane = lax.broadcasted_iota(jnp.int32, (c, LANES), 1)
    m_a = lane < RWKV_HEAD
    r_i = lax.broadcasted_iota(jnp.int32, (two_c, two_c), 0)
    c_i = lax.broadcasted_iota(jnp.int32, (two_c, two_c), 1)
    strict = r_i > c_i
    s_r = lax.broadcasted_iota(jnp.int32, (LANES, LANES), 0)
    s_c = lax.broadcasted_iota(jnp.int32, (LANES, LANES), 1)
    blk = (s_r < RWKV_HEAD) == (s_c < RWKV_HEAD)
    r_i2 = lax.broadcasted_iota(jnp.int32, (two_c, 2 * two_c), 0)
    c_i2 = lax.broadcasted_iota(jnp.int32, (two_c, 2 * two_c), 1) % two_c
    incl2 = r_i2 >= c_i2
    t_r = lax.broadcasted_iota(jnp.int32, (c, c), 0)
    t_c = lax.broadcasted_iota(jnp.int32, (c, c), 1)
    tri = (t_r >= t_c).astype(F32)

    def stack(x):
        return jnp.concatenate([jnp.where(m_a, x, 0.0), jnp.where(m_a, 0.0, x)], axis=0)

    def sub_chunk(ci, carry):
        rows = pl.ds(pl.multiple_of(ci * c, c), c)
        lw = lw_ref[0, rows, :]
        cl = _dot(tri, lw, precision=HIGHEST)
        cl_end = cl[c - 1:c, :]
        e_neg = jnp.exp(-cl)
        e_end = jnp.exp(cl_end - cl)
        w_end = jnp.exp(cl_end)
        aa = a_ref[0, rows, :].astype(F32) * jnp.exp(cl - lw)
        rr = r_ref[0, rows, :].astype(F32) * jnp.exp(cl)
        bb = b_ref[0, rows, :].astype(F32)
        kk = k_ref[0, rows, :].astype(F32)
        vv = v_ref[0, rows, :].astype(F32)
        pairs = range(n_pairs)
        sl = [slice(p * LANES, (p + 1) * LANES) for p in pairs]
        lhs = [jnp.concatenate([stack(aa[:, sl[p]]), stack(rr[:, sl[p]])], axis=0).astype(BF16) for p in pairs]
        rhs = [jnp.concatenate([stack((bb * e_neg)[:, sl[p]]), stack((kk * e_neg)[:, sl[p]])], axis=0).astype(BF16)
               for p in pairs]
        bk = [jnp.concatenate([stack((bb * e_end)[:, sl[p]]), stack((kk * e_end)[:, sl[p]])], axis=0).astype(BF16)
              for p in pairs]
        v_s = [stack(vv[:, sl[p]]) for p in pairs]
        s_old = [s_ref[p] for p in pairs]
        gmat = [_dot_nt(lhs[p], rhs[p]) for p in pairs]
        ph = [_dot_nt(lhs[p], s_old[p].astype(BF16)) for p in pairs]
        npow = [jnp.where(strict, gmat[p][:two_c, :two_c], 0.0).astype(BF16) for p in pairs]
        a_ak = [jnp.where(strict, gmat[p][:two_c, two_c:], 0.0).astype(BF16) for p in pairs]
        a_r = [jnp.where(incl2, gmat[p][two_c:, :], 0.0).astype(BF16) for p in pairs]
        u = [ph[p][:two_c] + _dot(a_ak[p], v_s[p].astype(BF16)) for p in pairs]
        for lvl in range(levels - 1):
            prod = [_dot(npow[p], jnp.concatenate([u[p].astype(BF16), npow[p]], axis=1)) for p in pairs]
            u = [u[p] + prod[p][:, :LANES] for p in pairs]
            npow = [prod[p][:, LANES:].astype(BF16) for p in pairs]
        u = [u[p] + _dot(npow[p], u[p].astype(BF16)) for p in pairs]
        uv = [jnp.concatenate([u[p], v_s[p]], axis=0).astype(BF16) for p in pairs]
        y_s = [ph[p][two_c:] + _dot(a_r[p], uv[p]) for p in pairs]
        y_ref[0, rows, :] = jnp.concatenate([y_s[p][:c] + y_s[p][c:] for p in pairs], axis=1).astype(y_ref.dtype)
        for p in pairs:
            s_new = s_old[p] * w_end[:, sl[p]] + _dot_tn(uv[p], bk[p])
            s_ref[p] = jnp.where(blk, s_new, 0.0)
        return carry

    lax.fori_loop(0, nsub, sub_chunk, 0)

    @pl.when(step == nsteps - 1)
    def _():
        s1_ref[0] = s_ref[...]


def _rwkv_scan(r, lw, k, v, an, b, s0):
    bsz, lp, _ = r.shape
    c = min(RWKV_CHUNK, lp)
    rows = min(lp, 512)
    nsteps = lp // rows
    nsub = rows // c
    kern = functools.partial(_rwkv_scan_kernel, c=c, nsub=nsub, nsteps=nsteps)
    seq = pl.BlockSpec((1, rows, D_MODEL), lambda i, s: (i, s, 0))
    st = pl.BlockSpec((1, RWKV_HEADS // 2, LANES, LANES), lambda i, s: (i, 0, 0, 0))
    return pl.pallas_call(
        kern,
        grid=(bsz, nsteps),
        in_specs=[seq] * 6 + [st],
        out_specs=[seq, st],
        out_shape=[jax.ShapeDtypeStruct((bsz, lp, D_MODEL), BF16),
                   jax.ShapeDtypeStruct((bsz, RWKV_HEADS // 2, LANES, LANES), F32)],
        scratch_shapes=[pltpu.VMEM((RWKV_HEADS // 2, LANES, LANES), F32)],
        compiler_params=_params("arbitrary", "arbitrary"),
        name="rwkv_scan",
    )(r, lw, k, v, an, b, s0)


def _route_top2(h, router_hi, router_lo):
    h_hi = h.astype(BF16)
    h_lo = (h - h_hi.astype(F32)).astype(BF16)
    logits = _dot(h_hi, router_hi) + (_dot(h_hi, router_lo) + _dot(h_lo, router_hi))
    lane = lax.broadcasted_iota(jnp.int32, logits.shape, 1)
    logits = jnp.where(lane < N_EXPERTS, logits, -jnp.inf)
    m1 = jnp.max(logits, axis=-1, keepdims=True)
    i1 = jnp.min(jnp.where(logits == m1, lane, LANES), axis=-1, keepdims=True)
    rest = jnp.where(lane == i1, -jnp.inf, logits)
    m2 = jnp.max(rest, axis=-1, keepdims=True)
    i2 = jnp.min(jnp.where(rest == m2, lane, LANES), axis=-1, keepdims=True)
    e2 = jnp.exp(m2 - m1)
    g1 = 1.0 / (1.0 + e2)
    g2 = e2 / (1.0 + e2)
    comb = jnp.where(lane == i1, g1, 0.0) + jnp.where(lane == i2, g2, 0.0)
    mask = jnp.where((lane == i1) | (lane == i2), 1.0, 0.0)
    return comb, mask


def _rwkv_out_kernel(x_ref, y_ref, r_ref, k_ref, v_ref, gate_ref, lnw_ref, lnb_ref, rk_ref, wo_ref, ones_ref, o_ref):
    ones_blk = ones_ref[...]
    inv = 1.0 / RWKV_HEAD
    parts = []
    for c in range(D_MODEL // MXU_DIM):
        sl = slice(c * MXU_DIM, (c + 1) * MXU_DIM)
        yh = y_ref[:, sl]
        y = yh.astype(F32)
        yc = y - _dot(yh, ones_blk) * inv
        var = _head_sums(yc * yc, ones_blk) * inv
        yn = yc * lax.rsqrt(var + RWKV_GN_EPS) * lnw_ref[:, sl] + lnb_ref[:, sl]
        rk = r_ref[:, sl].astype(F32) * k_ref[:, sl].astype(F32) * rk_ref[:, sl]
        rkh = rk.astype(BF16)
        bonus = _dot(rkh, ones_blk) + _head_sums(rk - rkh.astype(F32), ones_blk)
        parts.append(yn + bonus * v_ref[:, sl].astype(F32))
    out = jnp.concatenate(parts, axis=1) * gate_ref[...].astype(F32)
    o_ref[...] = x_ref[...] + _dot(out.astype(BF16), wo_ref[...])


def _rwkv_out(x, y, r, k, v, gate, lnw, lnb, rk, wo, ones_blk):
    t = x.shape[0]
    tm = min(TOKEN_TILE, t)
    row = pl.BlockSpec((tm, D_MODEL), lambda i: (i, 0))
    vec = _const_spec((1, D_MODEL))
    return pl.pallas_call(
        _rwkv_out_kernel,
        grid=(t // tm,),
        in_specs=[row] * 6 + [vec, vec, vec, _const_spec(wo.shape), _const_spec((MXU_DIM, MXU_DIM))],
        out_specs=row,
        out_shape=jax.ShapeDtypeStruct((t, D_MODEL), F32),
        compiler_params=_params("arbitrary"),
        name="rwkv_out",
    )(x, y, r, k, v, gate, lnw, lnb, rk, wo, ones_blk)


def _moe_route_kernel(x_ref, g_ref, rhi_ref, rlo_ref, comb_ref, mask_ref):
    comb_ref[...], mask_ref[...] = _route_top2(_rms_rows(x_ref[...], g_ref[...]), rhi_ref[...], rlo_ref[...])


def _moe_route(x, gain, router):
    t = x.shape[0]
    tm = min(TOKEN_TILE, t)
    row = lambda n: pl.BlockSpec((tm, n), lambda i: (i, 0))
    router_hi = router.astype(BF16)
    router_lo = (router - router_hi.astype(F32)).astype(BF16)
    return pl.pallas_call(
        _moe_route_kernel,
        grid=(t // tm,),
        in_specs=[row(D_MODEL), _const_spec((1, D_MODEL)), _const_spec((D_MODEL, LANES)),
                  _const_spec((D_MODEL, LANES))],
        out_specs=[row(LANES), row(LANES)],
        out_shape=[jax.ShapeDtypeStruct((t, LANES), F32), jax.ShapeDtypeStruct((t, LANES), F32)],
        compiler_params=_params("arbitrary"),
        name="moe_route",
    )(x, gain, router_hi, router_lo)


def _moe_plan(mask, comb, tm):
    m = mask[:, :N_EXPERTS].astype(jnp.int32)
    t = m.shape[0]
    tt = min(TOKEN_TILE, t)
    n_tiles = (2 * t) // tm + N_EXPERTS
    rank = jnp.cumsum(m, axis=0)
    counts = rank[-1]
    tiles_e = (counts + tm - 1) // tm
    tile_end = jnp.cumsum(tiles_e)
    tile_start = tile_end - tiles_e
    pos = (tile_start * tm)[None, :] + rank - 1
    slot = jnp.cumsum(m, axis=1) - 1
    c8 = comb[:, :N_EXPERTS]
    pick = [(slot == k) & (m > 0) for k in range(2)]
    pos_k = jnp.stack([jnp.sum(jnp.where(pk, pos, 0), axis=1) for pk in pick], axis=0)
    gates = jnp.stack([jnp.sum(jnp.where(pk, c8, 0.0), axis=1) for pk in pick], axis=1)
    pos_tab = pos_k.reshape(2, t // tt, tt).transpose(1, 0, 2).reshape(t // tt, 1, 2 * tt)
    tile_ids = jnp.arange(n_tiles, dtype=jnp.int32)
    texp = jnp.sum((tile_ids[:, None] >= tile_end[None, :]).astype(jnp.int32), axis=1)
    texp_c = jnp.minimum(texp, N_EXPERTS - 1)
    nvalid = jnp.clip(counts[texp_c] - (tile_ids - tile_start[texp_c]) * tm, 0, tm)
    nvalid = jnp.where(texp < N_EXPERTS, nvalid, 0).astype(jnp.int32)
    last_tile = jnp.concatenate([jnp.where(tiles_e > 0, tile_end - 1, -1), tile_end[-1:]]).astype(jnp.int32)
    return texp_c.astype(jnp.int32), nvalid, last_tile, pos_tab.astype(jnp.int32), gates, n_tiles


def _start_row_copies(pos_ref, tt, make):
    def body(j, c):
        for k in range(2):
            make(k, j, pos_ref[0, 0, k * tt + j]).start(priority=k)
        return c
    lax.fori_loop(0, tt, body, 0, unroll=8)


def _moe_dispatch_kernel(last_ref, pos_ref, x_ref, g_ref, xs_hbm, hbuf, zbuf, sem, zsem, *, tt, tm, n_tiles, nsteps):
    i = pl.program_id(0)

    def zero_tile(tile):
        return pltpu.make_async_copy(zbuf, xs_hbm.at[pl.ds(pl.multiple_of(tile * tm, tm), tm), :], zsem)

    @pl.when(i == 0)
    def _():
        zbuf[...] = jnp.zeros(zbuf.shape, F32)
        for e in range(N_EXPERTS):
            @pl.when(last_ref[e] >= 0)
            def _():
                zero_tile(last_ref[e]).start()
        for e in range(N_EXPERTS):
            @pl.when(last_ref[e] >= 0)
            def _():
                zero_tile(last_ref[e]).wait()

        def clear_unused(tile, c):
            zero_tile(tile).start()
            zero_tile(tile).wait()
            return c
        lax.fori_loop(last_ref[N_EXPERTS], n_tiles, clear_unused, 0)

    def drain(s):
        for _ in range(2):
            pltpu.make_async_copy(hbuf.at[s], xs_hbm.at[pl.ds(0, tt), :], sem.at[s]).wait()

    for s in range(2):
        @pl.when((i % 2 == s) & (i >= 2))
        def _():
            drain(s)

        @pl.when(i % 2 == s)
        def _():
            hbuf[s] = _rms_rows(x_ref[...], g_ref[...])
            _start_row_copies(pos_ref, tt, lambda k, j, dst: pltpu.make_async_copy(
                hbuf.at[s, pl.ds(j, 1), :], xs_hbm.at[pl.ds(dst, 1), :], sem.at[s]))

    @pl.when(i == nsteps - 1)
    def _():
        for s in range(min(2, nsteps)):
            drain(s)


def _moe_dispatch(last_tile, pos_tab, x, gain, n_rows, tm):
    nsteps, _, tt2 = pos_tab.shape
    tt = tt2 // 2
    kern = functools.partial(_moe_dispatch_kernel, tt=tt, tm=tm, n_tiles=n_rows // tm, nsteps=nsteps)
    return pl.pallas_call(
        kern,
        grid_spec=pltpu.PrefetchScalarGridSpec(
            num_scalar_prefetch=1,
            grid=(nsteps,),
            in_specs=[pl.BlockSpec((1, 1, tt2), lambda i, lt: (i, 0, 0), memory_space=pltpu.SMEM),
                      pl.BlockSpec((tt, D_MODEL), lambda i, lt: (i, 0)),
                      pl.BlockSpec((1, D_MODEL), lambda i, lt: (0, 0))],
            out_specs=pl.BlockSpec(memory_space=pl.ANY),
            scratch_shapes=[pltpu.VMEM((2, tt, D_MODEL), F32), pltpu.VMEM((tm, D_MODEL), F32),
                            pltpu.SemaphoreType.DMA((2,)), pltpu.SemaphoreType.DMA(())],
        ),
        out_shape=jax.ShapeDtypeStruct((n_rows, D_MODEL), F32),
        compiler_params=_params("arbitrary"),
        name="moe_dispatch",
    )(last_tile, pos_tab, x, gain)


def _moe_ffn_kernel(texp_ref, nvalid_ref, xs_ref, wg_ref, wu_ref, wd_ref, ys_ref, act_ref):
    n = nvalid_ref[pl.program_id(0)]

    @pl.when(n > 0)
    def _():
        _swiglu_act(xs_ref[...].astype(BF16), wg_ref.at[0], wu_ref.at[0], act_ref)
        ys_ref[...] = _dot(act_ref[...], wd_ref[0])

    @pl.when(n == 0)
    def _():
        ys_ref[...] = jnp.zeros(ys_ref.shape, F32)


def _moe_ffn(texp, nvalid, xs, wg, wu, wd, tm):
    n_tiles = xs.shape[0] // tm
    wspec = lambda s: pl.BlockSpec((1,) + s, lambda i, te, nv: (te[i], 0, 0))
    row = pl.BlockSpec((tm, D_MODEL), lambda i, te, nv: (i, 0))
    return pl.pallas_call(
        _moe_ffn_kernel,
        grid_spec=pltpu.PrefetchScalarGridSpec(
            num_scalar_prefetch=2,
            grid=(n_tiles,),
            in_specs=[row, wspec((D_MODEL, D_FF)), wspec((D_MODEL, D_FF)), wspec((D_FF, D_MODEL))],
            out_specs=row,
            scratch_shapes=[pltpu.VMEM((tm, D_FF), BF16)],
        ),
        out_shape=jax.ShapeDtypeStruct(xs.shape, F32),
        compiler_params=_params("arbitrary"),
        name="moe_ffn",
    )(texp, nvalid, xs, wg, wu, wd)


def _moe_combine_kernel(pos_ref, next_ref, x_ref, g_ref, ys_hbm, o_ref, ybuf, sem, *, tt, nsteps):
    i = pl.program_id(0)

    def fetch(table_ref, s):
        _start_row_copies(table_ref, tt, lambda k, j, src: pltpu.make_async_copy(
            ys_hbm.at[pl.ds(src, 1), :], ybuf.at[s, k, pl.ds(j, 1), :], sem.at[s]))

    @pl.when(i == 0)
    def _():
        fetch(pos_ref, 0)

    for s in range(2):
        @pl.when((i % 2 == s) & (i + 1 < nsteps))
        def _():
            fetch(next_ref, 1 - s)

        @pl.when(i % 2 == s)
        def _():
            for k in range(2):
                pltpu.make_async_copy(ys_hbm.at[pl.ds(0, tt), :], ybuf.at[s, k], sem.at[s]).wait()
            g = g_ref[...]
            o_ref[...] = x_ref[...] + (g[:, 0:1] * ybuf[s, 0] + g[:, 1:2] * ybuf[s, 1])


def _moe_combine(pos_tab, x, gates, ys):
    t = x.shape[0]
    nsteps, _, tt2 = pos_tab.shape
    tt = tt2 // 2
    kern = functools.partial(_moe_combine_kernel, tt=tt, nsteps=nsteps)
    row = lambda n: pl.BlockSpec((tt, n), lambda i: (i, 0))
    table = lambda off: pl.BlockSpec((1, 1, tt2), lambda i: (jnp.minimum(i + off, nsteps - 1), 0, 0),
                                     memory_space=pltpu.SMEM)
    return pl.pallas_call(
        kern,
        grid=(nsteps,),
        in_specs=[table(0), table(1), row(D_MODEL), row(2), pl.BlockSpec(memory_space=pl.ANY)],
        out_specs=row(D_MODEL),
        out_shape=jax.ShapeDtypeStruct((t, D_MODEL), F32),
        scratch_shapes=[pltpu.VMEM((2, 2, tt, D_MODEL), F32), pltpu.SemaphoreType.DMA((2,))],
        compiler_params=_params("arbitrary"),
        name="moe_combine",
    )(pos_tab, pos_tab, x, gates, ys)


def _moe(x, gain, router, wg, wu, wd):
    t = x.shape[0]
    tm = TOKEN_TILE if 2 * t >= N_EXPERTS * 4 * TOKEN_TILE else TOKEN_TILE // 2
    comb, mask = _moe_route(x, gain, router)
    texp, nvalid, last_tile, pos_tab, gates, n_tiles = _moe_plan(mask, comb, tm)
    xs = _moe_dispatch(last_tile, pos_tab, x, gain, n_tiles * tm, tm)
    ys = _moe_ffn(texp, nvalid, xs, wg, wu, wd, tm)
    return _moe_combine(pos_tab, x, gates, ys)


def _pair_states(s):
    b = s.shape[0]
    s = s.reshape(b, RWKV_HEADS // 2, 2, RWKV_HEAD, RWKV_HEAD)
    z = jnp.zeros_like(s[:, :, 0])
    top = jnp.concatenate([s[:, :, 0], z], axis=-1)
    bot = jnp.concatenate([z, s[:, :, 1]], axis=-1)
    return jnp.concatenate([top, bot], axis=-2)


def _unpair_states(s):
    b = s.shape[0]
    s0 = s[:, :, :RWKV_HEAD, :RWKV_HEAD]
    s1 = s[:, :, RWKV_HEAD:, RWKV_HEAD:]
    return jnp.stack([s0, s1], axis=2).reshape(b, RWKV_HEADS, RWKV_HEAD, RWKV_HEAD)


def _trunk(x, ssm, conv, ck, cv, wkv, shift, w):
    b, l, _ = x.shape
    t = b * l
    ones_blk = w["ones_blk"]
    xf = x.reshape(t, D_MODEL)

    z, xbc, dt, q, k, v = _inproj(xf, w["norm_mix0"], w["wz"], w["wxbc"], w["wdt"], w["wq"], w["wk"], w["wv"],
                                  w["q_norm"], w["k_norm"], ones_blk)
    lp = l if l % SSD_CHUNK == 0 else 8
    seq = lambda a: a.reshape(b, l, a.shape[-1])
    padl = lambda a: a if lp == l else jnp.pad(a, ((0, 0), (0, lp - l), (0, 0)))
    y_ssd, ssm1, conv1 = _ssd(padl(seq(xbc)), padl(seq(dt)), padl(seq(z)), conv, ssm.reshape(b, SSD_WIDTH, SSD_STATE),
                              w["conv_w"], w["conv_b"], w["dt_bias"], w["a_log"], w["d_skip"], w["ssd_norm"], lv=l)
    y_ssd = y_ssd[:, :l].reshape(t, SSD_WIDTH)
    ssm1 = ssm1.reshape(b, SSD_HEADS, SSD_HEAD_DIM, SSD_STATE)
    if ck is None:
        o = _swa_prompt(seq(q), seq(k), seq(v), w["sinks"]).reshape(t, Q_DIM)
        k1 = seq(k)[:, -WINDOW:].reshape(b, WINDOW, ATT_KV_HEADS, ATT_HEAD_DIM)
        v1 = seq(v)[:, -WINDOW:].reshape(b, WINDOW, ATT_KV_HEADS, ATT_HEAD_DIM)
    else:
        wc = ck.shape[1]
        o, k1, v1 = _swa_cached(q, k, v, ck.reshape(b, wc, KV_DIM), cv.reshape(b, wc, KV_DIM), w["sinks"], l=l)
        k1 = k1.reshape(b, wc, ATT_KV_HEADS, ATT_HEAD_DIM)
        v1 = v1.reshape(b, wc, ATT_KV_HEADS, ATT_HEAD_DIM)
    x2 = _outproj_ffn(xf, y_ssd, o, w["wo_a"], w["wo_b"], w["norm_ffn0"], w["ffn_gate"], w["ffn_up"], w["ffn_down"])

    tm = min(TOKEN_TILE, t)
    if l >= tm:
        first = shift.reshape(b, 1, D_MODEL)
    else:
        first = jnp.repeat(shift, l, axis=0)
    r, lw, kx, vx, an, bb, gate, h1 = _rwkv_prep(
        x2, first, w["norm_mix1"], w["mu"], w["w0"], w["a0"], w["kk"], w["ka"], w["wr"], w["wkk"], w["wvv"],
        w["w1"], w["w2"], w["a1"], w["a2"], w["g1"], w["g2"], ones_blk, seq_len=l)
    shift1 = h1.reshape(b, D_MODEL) if l >= tm else h1.reshape(b, l, D_MODEL)[:, -1]
    if l >= RWKV_CHUNK:
        lpr = -(-l // RWKV_CHUNK) * RWKV_CHUNK
    else:
        lpr = max(RWKV_SHORT_CHUNK, pl.next_power_of_2(l))
    padr = lambda a: seq(a) if lpr == l else jnp.pad(seq(a), ((0, 0), (0, lpr - l), (0, 0)))
    y, s1 = _rwkv_scan(padr(r), padr(lw), padr(kx), padr(vx), padr(an), padr(bb), _pair_states(wkv))
    y = y[:, :l].reshape(t, D_MODEL)
    x3 = _rwkv_out(x2, y, r, kx, vx, gate, w["ln_w"], w["ln_b"], w["rk"], w["wo"], ones_blk)
    x4 = _moe(x3, w["norm_ffn1"], w["router"], w["moe_gate"], w["moe_up"], w["moe_down"])
    return (x4.reshape(b, l, D_MODEL), ssm1[None], conv1[None], k1[None], v1[None],
            _unpair_states(s1)[None], shift1[None])


def kernel(x_prompt, x_sample, state_ssm, state_conv, cache_swa_k, cache_swa_v, state_wkv, state_shift, norm_mix, norm_ffn, w_in, conv_w, conv_b, dt_bias, a_log, d_skip, ssd_norm, q_norm, k_norm, attn_sinks, w_out, ffn_gate, ffn_up, ffn_down, rwkv_mu, rwkv_w0, rwkv_w1, rwkv_w2, rwkv_a0, rwkv_a1, rwkv_a2, rwkv_g1, rwkv_g2, rwkv_kk, rwkv_ka, rwkv_rk, rwkv_wr, rwkv_wk, rwkv_wv, rwkv_wo, rwkv_ln_w, rwkv_ln_b, moe_router, moe_gate, moe_up, moe_down):
    bf = lambda a: a.astype(BF16)
    row = lambda a: a.reshape(1, -1).astype(F32)
    padlane = lambda a: jnp.pad(a, ((0, 0), (0, LANES - a.shape[1])))
    wi = w_in[0]
    c0 = SSD_WIDTH
    c1 = c0 + CONV_DIM
    c2 = c1 + SSD_HEADS
    c3 = c2 + Q_DIM
    c4 = c3 + KV_DIM
    w = dict(
        ones_blk=_block_ones(),
        norm_mix0=row(norm_mix[0]), norm_mix1=row(norm_mix[1]), norm_ffn0=row(norm_ffn[0]), norm_ffn1=row(norm_ffn[1]),
        wz=bf(wi[:, :c0]), wxbc=bf(wi[:, c0:c1]), wdt=bf(padlane(wi[:, c1:c2])), wq=bf(wi[:, c2:c3]),
        wk=bf(wi[:, c3:c4]), wv=bf(wi[:, c4:]),
        q_norm=row(jnp.tile(q_norm[0], ATT_HEADS)), k_norm=row(jnp.tile(k_norm[0], ATT_KV_HEADS)),
        conv_w=conv_w[0], conv_b=row(conv_b[0]), dt_bias=padlane(row(dt_bias[0])), a_log=padlane(row(a_log[0])),
        d_skip=row(jnp.repeat(d_skip[0], SSD_HEAD_DIM)), ssd_norm=row(ssd_norm[0]), sinks=attn_sinks[0].astype(F32),
        wo_a=bf(w_out[0, :SSD_WIDTH]), wo_b=bf(w_out[0, SSD_WIDTH:]),
        ffn_gate=bf(ffn_gate[0]), ffn_up=bf(ffn_up[0]), ffn_down=bf(ffn_down[0]),
        mu=rwkv_mu[0], w0=row(rwkv_w0[0]), a0=row(rwkv_a0[0]), kk=row(rwkv_kk[0]), ka=row(rwkv_ka[0]),
        wr=bf(rwkv_wr[0]), wkk=bf(rwkv_wk[0]), wvv=bf(rwkv_wv[0]), wo=bf(rwkv_wo[0]),
        w1=bf(rwkv_w1[0]), w2=bf(rwkv_w2[0]), a1=bf(rwkv_a1[0]), a2=bf(rwkv_a2[0]),
        g1=bf(rwkv_g1[0]), g2=bf(rwkv_g2[0]),
        ln_w=row(rwkv_ln_w[0]), ln_b=row(rwkv_ln_b[0]), rk=row(rwkv_rk[0]),
        router=padlane(moe_router[0]), moe_gate=bf(moe_gate[0]), moe_up=bf(moe_up[0]), moe_down=bf(moe_down[0]),
    )
    bp = x_prompt.shape[0]
    z_ssm = jnp.zeros((bp,) + state_ssm.shape[2:], F32)
    z_conv = jnp.zeros((bp,) + state_conv.shape[2:], F32)
    z_wkv = jnp.zeros((bp,) + state_wkv.shape[2:], F32)
    z_shift = jnp.zeros((bp,) + state_shift.shape[2:], F32)
    outs_p = _trunk(x_prompt, z_ssm, z_conv, None, None, z_wkv, z_shift, w)
    outs_s = _trunk(x_sample, state_ssm[0], state_conv[0], cache_swa_k[0], cache_swa_v[0], state_wkv[0],
                    state_shift[0], w)
    return (outs_p[0], outs_s[0]) + tuple(outs_p[1:]) + tuple(outs_s[1:])
```

```python
import functools

import jax
import jax.numpy as jnp
from jax import lax
from jax.experimental import pallas as pl
from jax.experimental.pallas import tpu as pltpu

F32 = jnp.float32
BF16 = jnp.bfloat16

D_MODEL = 1024
SSD_HEAD_DIM = 64
SSD_HEADS = 16
SSD_GROUPS = 2
SSD_STATE = 128
SSD_CONV = 4
SSD_CHUNK = 128
SSD_SHORT_CHUNK = 16
SSD_WIDTH = 1024
CONV_DIM = SSD_WIDTH + 2 * SSD_GROUPS * SSD_STATE
ATT_HEAD_DIM = 64
ATT_HEADS = 16
ATT_KV_HEADS = 4
ATT_REP = ATT_HEADS // ATT_KV_HEADS
WINDOW = 128
Q_DIM = ATT_HEADS * ATT_HEAD_DIM
KV_DIM = ATT_KV_HEADS * ATT_HEAD_DIM
RWKV_HEAD = 64
RWKV_HEADS = 16
RWKV_GN_EPS = 64e-5
RWKV_CHUNK = 64
RWKV_SHORT_CHUNK = 16
D_FF = 2816
N_EXPERTS = 8
RMS_EPS = 1e-6
NEG_INF = -1e30

LANES = 128
MXU_DIM = 256
VMEM_LIMIT_BYTES = 56 * 1024 * 1024
TOKEN_TILE = 512
FF_CHUNK = MXU_DIM
HIGHEST = lax.Precision.HIGHEST


def _dot(a, b, precision=None):
    return jnp.dot(a, b, preferred_element_type=F32, precision=precision)


def _dot_nt(a, b, precision=None):
    return lax.dot_general(a, b, (((1,), (1,)), ((), ())), preferred_element_type=F32, precision=precision)


def _dot_tn(a, b, precision=None):
    return lax.dot_general(a, b, (((0,), (0,)), ((), ())), preferred_element_type=F32, precision=precision)


def _rms_rows(x, g):
    return x * lax.rsqrt(jnp.mean(x * x, axis=-1, keepdims=True) + RMS_EPS) * g


def _sigmoid(x):
    return 1.0 / (1.0 + jnp.exp(-x))


def _silu(x):
    return x * _sigmoid(x)


def _softplus(x):
    return jnp.maximum(x, 0.0) + jnp.log(1.0 + jnp.exp(-jnp.abs(x)))


def _const_spec(shape):
    zeros = (0,) * len(shape)
    return pl.BlockSpec(shape, lambda *_: zeros, pipeline_mode=pl.Buffered(1))


def _params(*semantics):
    return pltpu.CompilerParams(dimension_semantics=semantics, vmem_limit_bytes=VMEM_LIMIT_BYTES)


def _head_sums(t, ones_blk):
    return _dot(t.astype(BF16), ones_blk)


def _block_ones():
    r = jnp.arange(MXU_DIM) // ATT_HEAD_DIM
    return (r[:, None] == r[None, :]).astype(BF16)


def _inproj_kernel(x_ref, g_ref, wz_ref, wxbc_ref, wdt_ref, wq_ref, wk_ref, wv_ref, qn_ref, kn_ref, ones_ref,
                   z_ref, xbc_ref, dt_ref, q_ref, k_ref, v_ref):
    h = _rms_rows(x_ref[...], g_ref[...]).astype(BF16)
    ones_blk = ones_ref[...]

    def head_rms(t, gain):
        outs = []
        for c in range(t.shape[1] // MXU_DIM):
            tc = t[:, c * MXU_DIM:(c + 1) * MXU_DIM]
            ms = _head_sums(tc * tc, ones_blk) * (1.0 / ATT_HEAD_DIM)
            outs.append(tc * lax.rsqrt(ms + RMS_EPS))
        return jnp.concatenate(outs, axis=1) * gain

    z_ref[...] = _dot(h, wz_ref[...]).astype(z_ref.dtype)
    xbc_ref[...] = _dot(h, wxbc_ref[...])
    dt_ref[...] = _dot(h, wdt_ref[...])
    q_ref[...] = head_rms(_dot(h, wq_ref[...]), qn_ref[...]).astype(q_ref.dtype)
    k_ref[...] = head_rms(_dot(h, wk_ref[...]), kn_ref[...])
    v_ref[...] = _dot(h, wv_ref[...])


def _inproj(x, gain, wz, wxbc, wdt, wq, wk, wv, qn, kn, ones_blk):
    t = x.shape[0]
    tm = min(TOKEN_TILE, t)
    row = lambda n: pl.BlockSpec((tm, n), lambda i: (i, 0))
    return pl.pallas_call(
        _inproj_kernel,
        grid=(t // tm,),
        in_specs=[row(D_MODEL), _const_spec((1, D_MODEL)),
                  _const_spec(wz.shape), _const_spec(wxbc.shape), _const_spec(wdt.shape),
                  _const_spec(wq.shape), _const_spec(wk.shape), _const_spec(wv.shape),
                  _const_spec((1, Q_DIM)), _const_spec((1, KV_DIM)), _const_spec((MXU_DIM, MXU_DIM))],
        out_specs=[row(SSD_WIDTH), row(CONV_DIM), row(LANES), row(Q_DIM), row(KV_DIM), row(KV_DIM)],
        out_shape=[jax.ShapeDtypeStruct((t, SSD_WIDTH), BF16), jax.ShapeDtypeStruct((t, CONV_DIM), F32),
                   jax.ShapeDtypeStruct((t, LANES), F32), jax.ShapeDtypeStruct((t, Q_DIM), BF16),
                   jax.ShapeDtypeStruct((t, KV_DIM), F32), jax.ShapeDtypeStruct((t, KV_DIM), F32)],
        compiler_params=_params("arbitrary"),
        name="inproj",
    )(x, gain, wz, wxbc, wdt, wq, wk, wv, qn, kn, ones_blk)


def _ssd_kernel(xbc_ref, dt_ref, z_ref, conv0_ref, ssm0_ref, cw_ref, cb_ref, dtb_ref, alog_ref, dsk_ref, nrm_ref,
                y_ref, ssm1_ref, conv1_ref, xpad_ref, dtpad_ref, s_ref, *, q, lin, lv, nchunks):
    c = pl.program_id(1)
    n_pairs = SSD_HEADS // 2
    hist = SSD_CONV - 1
    base = 8 - hist

    @pl.when(c == 0)
    def _():
        xpad_ref[...] = jnp.zeros(xpad_ref.shape, F32)
        dtpad_ref[...] = jnp.zeros(dtpad_ref.shape, F32)
        xpad_ref[base:8, :] = conv0_ref[0]
        for jb in range(n_pairs):
            s_ref[:, jb * LANES:(jb + 1) * LANES] = ssm0_ref[0, jb * LANES:(jb + 1) * LANES, :].T

    xpad_ref[8:8 + lin, :] = xbc_ref[0]
    dtpad_ref[0:lin, :] = dt_ref[0]

    conv = cb_ref[...]
    for j in range(SSD_CONV):
        conv = conv + xpad_ref[base + j:base + j + q, :] * cw_ref[j:j + 1, :]
    tail = xpad_ref[base + lv:8 + lv, :]
    conv1_ref[0] = tail
    xpad_ref[base:8, :] = tail

    xc = _silu(conv)
    xs = xc[:, :SSD_WIDTH]
    bm = xc[:, SSD_WIDTH:SSD_WIDTH + SSD_GROUPS * SSD_STATE].astype(BF16)
    cm = xc[:, SSD_WIDTH + SSD_GROUPS * SSD_STATE:].astype(BF16)

    row_i = lax.broadcasted_iota(jnp.int32, (q, LANES), 0)
    col_i = lax.broadcasted_iota(jnp.int32, (q, LANES), 1)
    first_half = col_i < SSD_HEAD_DIM
    dt = jnp.where(row_i < lv, _softplus(dtpad_ref[...] + dtb_ref[...]), 0.0)
    a = dt * (-jnp.exp(alog_ref[...]))
    r_q = lax.broadcasted_iota(jnp.int32, (q, q), 0)
    c_q = lax.broadcasted_iota(jnp.int32, (q, q), 1)
    causal = r_q >= c_q
    a_cs = _dot(causal.astype(F32), a, precision=HIGHEST)
    if q == LANES:
        a_cs_t = a_cs.T
    else:
        a_cs_t = jnp.concatenate([a_cs, jnp.zeros((LANES - q, LANES), F32)], axis=0).T[:, :q]
    ea = jnp.exp(a_cs)
    dte = jnp.exp(a_cs[q - 1:q, :] - a_cs)

    def colb(arr, h):
        return jnp.broadcast_to(arr[:, h:h + 1], (q, LANES))

    def pairb(arr, h0):
        return jnp.where(first_half, colb(arr, h0), colb(arr, h0 + 1))

    cbs = []
    for g in range(SSD_GROUPS):
        sl = slice(g * SSD_STATE, (g + 1) * SSD_STATE)
        cbs.append(_dot_nt(cm[:, sl], bm[:, sl]))

    ys = []
    for j in range(n_pairs):
        g = (2 * j) // (SSD_HEADS // SSD_GROUPS)
        gsl = slice(g * SSD_STATE, (g + 1) * SSD_STATE)
        psl = slice(j * LANES, (j + 1) * LANES)
        xs_p = xs[:, psl]
        xdt = xs_p * pairb(dt, 2 * j)
        xdt_b = xdt.astype(BF16)
        yd = []
        for hh in (2 * j, 2 * j + 1):
            diff = colb(a_cs, hh)[:, :q] - a_cs_t[hh:hh + 1, :]
            lmat = jnp.exp(jnp.where(causal, diff, NEG_INF))
            yd.append(_dot((cbs[g] * lmat).astype(BF16), xdt_b))
        y_diag = jnp.where(first_half, yd[0], yd[1])
        ea_p = pairb(ea, 2 * j)
        s_old = s_ref[:, psl]
        y_off = _dot(cm[:, gsl], s_old.astype(BF16)) * ea_p
        s_ref[:, psl] = s_old * ea_p[q - 1:q, :] + _dot_tn(bm[:, gsl], (xdt * pairb(dte, 2 * j)).astype(BF16))
        ys.append(y_diag + y_off + xs_p * dsk_ref[:, psl])
    y = jnp.concatenate(ys, axis=1)[0:lin]

    y = y * _silu(z_ref[0].astype(F32))
    gw = SSD_WIDTH // SSD_GROUPS
    outs = []
    for g in range(SSD_GROUPS):
        yg = y[:, g * gw:(g + 1) * gw]
        outs.append(yg * lax.rsqrt(jnp.mean(yg * yg, axis=-1, keepdims=True) + RMS_EPS))
    y_ref[0] = (jnp.concatenate(outs, axis=1) * nrm_ref[...]).astype(y_ref.dtype)

    @pl.when(c == nchunks - 1)
    def _():
        for jb in range(n_pairs):
            ssm1_ref[0, jb * LANES:(jb + 1) * LANES, :] = s_ref[:, jb * LANES:(jb + 1) * LANES].T


def _ssd(xbc, dt, z, conv0, ssm0, cw, cb, dtb, alog, dsk, nrm, *, lv):
    b, lp, _ = xbc.shape
    q = SSD_CHUNK if lp >= SSD_CHUNK else SSD_SHORT_CHUNK
    lin = min(q, lp)
    nchunks = lp // lin
    lv_chunk = min(lv, lin)
    kern = functools.partial(_ssd_kernel, q=q, lin=lin, lv=lv_chunk, nchunks=nchunks)
    seq = lambda n: pl.BlockSpec((1, lin, n), lambda i, c: (i, c, 0))
    per_b = lambda s: pl.BlockSpec((1,) + s, lambda i, c: (i, 0, 0))
    return pl.pallas_call(
        kern,
        grid=(b, nchunks),
        in_specs=[seq(CONV_DIM), seq(LANES), seq(SSD_WIDTH),
                  per_b((SSD_CONV - 1, CONV_DIM)), per_b((SSD_WIDTH, SSD_STATE)),
                  _const_spec((SSD_CONV, CONV_DIM)), _const_spec((1, CONV_DIM)), _const_spec((1, LANES)),
                  _const_spec((1, LANES)), _const_spec((1, SSD_WIDTH)), _const_spec((1, SSD_WIDTH))],
        out_specs=[seq(SSD_WIDTH), per_b((SSD_WIDTH, SSD_STATE)), per_b((SSD_CONV - 1, CONV_DIM))],
        out_shape=[jax.ShapeDtypeStruct((b, lp, SSD_WIDTH), BF16),
                   jax.ShapeDtypeStruct((b, SSD_WIDTH, SSD_STATE), F32),
                   jax.ShapeDtypeStruct((b, SSD_CONV - 1, CONV_DIM), F32)],
        scratch_shapes=[pltpu.VMEM((q + 8, CONV_DIM), F32), pltpu.VMEM((q, LANES), F32),
                        pltpu.VMEM((SSD_STATE, SSD_WIDTH), F32)],
        compiler_params=_params("arbitrary", "arbitrary"),
        name="ssd",
    )(xbc, dt, z, conv0, ssm0, cw, cb, dtb, alog, dsk, nrm)


def _sink_softmax(s, allowed, sink_col):
    s = jnp.where(allowed, s, NEG_INF)
    m = jnp.maximum(jnp.max(s, axis=-1, keepdims=True), sink_col)
    return jnp.exp(s - m).astype(BF16), jnp.exp(sink_col - m)


def _gqa_blocks(blocks, sink_ref, allowed, nq, group):
    d = ATT_HEAD_DIM
    sinks = [jnp.concatenate([jnp.full((nq, 1), sink_ref[ATT_REP * j + r], F32) for r in range(ATT_REP)], axis=0)
             for j in range(ATT_KV_HEADS)]
    problems = [(bi, j) for bi in range(len(blocks)) for j in range(ATT_KV_HEADS)]
    ones_k = jnp.ones((blocks[0][1].shape[0], d), BF16)
    outs = {}
    for g0 in range(0, len(problems), group):
        grp = problems[g0:g0 + group]
        scores = []
        for bi, j in grp:
            qb, kk, _ = blocks[bi]
            qj = jnp.concatenate(
                [qb[:, (ATT_REP * j + r) * d:(ATT_REP * j + r + 1) * d] for r in range(ATT_REP)], axis=0)
            scores.append(_dot_nt(qj * (d ** -0.5), kk[:, j * d:(j + 1) * d]))
        probs = [_sink_softmax(s, allowed, sinks[j]) for s, (_, j) in zip(scores, grp)]
        for (p, sink_term), (bi, j) in zip(probs, grp):
            denom = _dot(p, ones_k) + sink_term
            outs[bi, j] = _dot(p, blocks[bi][2][:, j * d:(j + 1) * d]) / denom
    return [jnp.concatenate([outs[bi, j][r * nq:(r + 1) * nq] for j in range(ATT_KV_HEADS) for r in range(ATT_REP)],
                            axis=1) for bi in range(len(blocks))]


def _swa_prompt_kernel(sink_ref, q_ref, kp_ref, kc_ref, vp_ref, vc_ref, o_ref):
    i = pl.program_id(1)
    w = WINDOW
    kk = jnp.concatenate([kp_ref[0], kc_ref[0]], axis=0).astype(BF16)
    vv = jnp.concatenate([vp_ref[0], vc_ref[0]], axis=0).astype(BF16)
    rows = lax.broadcasted_iota(jnp.int32, (ATT_REP * w, 2 * w), 0) % w
    cols = lax.broadcasted_iota(jnp.int32, (ATT_REP * w, 2 * w), 1)
    rel = rows + w - cols
    allowed = (rel >= 0) & (rel <= w) & ((cols >= w) | (i > 0))
    o_ref[0] = _gqa_blocks([(q_ref[0], kk, vv)], sink_ref, allowed, w, group=ATT_KV_HEADS)[0].astype(o_ref.dtype)


def _swa_prompt(q, k, v, sinks):
    b, l, _ = q.shape
    nb = l // WINDOW
    cur = lambda n: pl.BlockSpec((1, WINDOW, n), lambda i, c: (i, c, 0))
    prev = lambda n: pl.BlockSpec((1, WINDOW, n), lambda i, c: (i, jnp.maximum(c - 1, 0), 0))
    return pl.pallas_call(
        _swa_prompt_kernel,
        grid=(b, nb),
        in_specs=[pl.BlockSpec(memory_space=pltpu.SMEM),
                  cur(Q_DIM), prev(KV_DIM), cur(KV_DIM), prev(KV_DIM), cur(KV_DIM)],
        out_specs=cur(Q_DIM),
        out_shape=jax.ShapeDtypeStruct((b, l, Q_DIM), BF16),
        compiler_params=_params("arbitrary", "arbitrary"),
        name="swa_prompt",
    )(sinks, q, k, k, v, v)


def _swa_cached_kernel(sink_ref, q_ref, k_ref, v_ref, kbuf_ref, vbuf_ref, o_ref, k1_ref, v1_ref, *, nbatch, l):
    wc = kbuf_ref.shape[1]
    pad = 8 - l
    nk = wc + 8
    rows = lax.broadcasted_iota(jnp.int32, (ATT_REP * l, nk), 0) % l
    cols = lax.broadcasted_iota(jnp.int32, (ATT_REP * l, nk), 1)
    rel = rows + wc - cols
    allowed = (rel >= 0) & (rel <= WINDOW)
    zpad = jnp.zeros((pad, KV_DIM), F32)
    qall = q_ref[...].astype(F32)
    blocks = []
    for bi in range(nbatch):
        kn = k_ref[bi * l:(bi + 1) * l, :]
        vn = v_ref[bi * l:(bi + 1) * l, :]
        kc = jnp.concatenate([kbuf_ref[bi], kn, zpad], axis=0).astype(BF16)
        vc = jnp.concatenate([vbuf_ref[bi], vn, zpad], axis=0).astype(BF16)
        blocks.append((qall[bi * l:(bi + 1) * l, :].astype(BF16), kc, vc))
        k1_ref[bi, 0:wc - l, :] = kbuf_ref[bi, l:wc, :]
        k1_ref[bi, wc - l:wc, :] = kn
        v1_ref[bi, 0:wc - l, :] = vbuf_ref[bi, l:wc, :]
        v1_ref[bi, wc - l:wc, :] = vn
    o_ref[...] = jnp.concatenate(_gqa_blocks(blocks, sink_ref, allowed, l, group=ATT_KV_HEADS * nbatch), axis=0).astype(o_ref.dtype)


def _swa_cached(q, k, v, kbuf, vbuf, sinks, *, l):
    b, wc, _ = kbuf.shape
    nbatch = 8
    kern = functools.partial(_swa_cached_kernel, nbatch=nbatch, l=l)
    rows = lambda n: pl.BlockSpec((nbatch * l, n), lambda i: (i, 0))
    cache = pl.BlockSpec((nbatch, wc, KV_DIM), lambda i: (i, 0, 0))
    return pl.pallas_call(
        kern,
        grid=(b // nbatch,),
        in_specs=[pl.BlockSpec(memory_space=pltpu.SMEM), rows(Q_DIM), rows(KV_DIM), rows(KV_DIM), cache, cache],
        out_specs=[rows(Q_DIM), cache, cache],
        out_shape=[jax.ShapeDtypeStruct((b * l, Q_DIM), BF16),
                   jax.ShapeDtypeStruct((b, wc, KV_DIM), F32), jax.ShapeDtypeStruct((b, wc, KV_DIM), F32)],
        compiler_params=_params("arbitrary"),
        name="swa_cached",
    )(sinks, q, k, v, kbuf, vbuf)


def _swiglu_act(h, wg_ref, wu_ref, act_ref):
    for c in range(D_FF // FF_CHUNK):
        sl = slice(c * FF_CHUNK, (c + 1) * FF_CHUNK)
        act_ref[:, sl] = (_silu(_dot(h, wg_ref[:, sl])) * _dot(h, wu_ref[:, sl])).astype(BF16)


def _outproj_ffn_kernel(x_ref, ya_ref, yb_ref, woa_ref, wob_ref, g_ref, wg_ref, wu_ref, wd_ref, o_ref, act_ref):
    x1 = x_ref[...] + _dot(ya_ref[...], woa_ref[...]) + _dot(yb_ref[...], wob_ref[...])
    h = _rms_rows(x1, g_ref[...]).astype(BF16)
    _swiglu_act(h, wg_ref, wu_ref, act_ref)
    o_ref[...] = x1 + _dot(act_ref[...], wd_ref[...])


def _outproj_ffn(x, ya, yb, woa, wob, gain, wg, wu, wd):
    t = x.shape[0]
    tm = min(TOKEN_TILE, t)
    row = lambda n: pl.BlockSpec((tm, n), lambda i: (i, 0))
    return pl.pallas_call(
        _outproj_ffn_kernel,
        grid=(t // tm,),
        in_specs=[row(D_MODEL), row(SSD_WIDTH), row(Q_DIM), _const_spec(woa.shape), _const_spec(wob.shape),
                  _const_spec((1, D_MODEL)), _const_spec(wg.shape), _const_spec(wu.shape), _const_spec(wd.shape)],
        out_specs=row(D_MODEL),
        out_shape=jax.ShapeDtypeStruct((t, D_MODEL), F32),
        scratch_shapes=[pltpu.VMEM((tm, D_FF), BF16)],
        compiler_params=_params("arbitrary"),
        name="outproj_ffn",
    )(x, ya, yb, woa, wob, gain, wg, wu, wd)


def _rwkv_prep_kernel(x_ref, first_ref, g_ref, mu_ref, w0_ref, a0_ref, kkc_ref, ka_ref,
                      wr_ref, wk_ref, wv_ref, w1_ref, w2_ref, a1_ref, a2_ref, g1_ref, g2_ref, ones_ref,
                      r_ref, lw_ref, k_ref, v_ref, an_ref, b_ref, gate_ref, h_ref, hs_ref, *, tm, seq_len):
    i = pl.program_id(0)
    h = _rms_rows(x_ref[...], g_ref[...])
    tiles_per_seq = max(seq_len // tm, 1)

    if seq_len >= tm:
        h_ref[0] = h[tm - 1:tm, :]

        @pl.when(i % tiles_per_seq == 0)
        def _():
            hs_ref[7:8, :] = first_ref[0]
    else:
        h_ref[...] = h
        hs_ref[0:8, :] = jnp.zeros((8, D_MODEL), F32)
    hs_ref[8:8 + tm, :] = h
    prev = hs_ref[7:7 + tm, :]
    if seq_len < tm:
        rows = lax.broadcasted_iota(jnp.int32, (tm, D_MODEL), 0)
        prev = jnp.where(rows % seq_len == 0, first_ref[...], prev)
    else:
        hs_ref[7:8, :] = h[tm - 1:tm, :]
    xx = prev - h
    mix = lambda n: (h + xx * mu_ref[n:n + 1, :]).astype(BF16)
    r = _dot(mix(0), wr_ref[...])
    k = _dot(mix(2), wk_ref[...])
    v = _dot(mix(3), wv_ref[...])
    w_lora = _dot(jnp.tanh(_dot(mix(1), w1_ref[...])).astype(BF16), w2_ref[...])
    a_lora = _dot(_dot(mix(4), a1_ref[...]).astype(BF16), a2_ref[...])
    gate = _dot(_sigmoid(_dot(mix(5), g1_ref[...])).astype(BF16), g2_ref[...])
    neg_decay_log = -(2.718281828459045 ** -0.5) * _sigmoid(w0_ref[...] + w_lora)
    a_sig = _sigmoid(a0_ref[...] + a_lora)
    kk = k * kkc_ref[...]
    ones_blk = ones_ref[...]
    parts = []
    for c in range(D_MODEL // MXU_DIM):
        kc = kk[:, c * MXU_DIM:(c + 1) * MXU_DIM]
        nrm = jnp.maximum(jnp.sqrt(_head_sums(kc * kc, ones_blk)), 1e-12)
        parts.append(kc / nrm)
    kk = jnp.concatenate(parts, axis=1)
    r_ref[...] = r.astype(r_ref.dtype)
    lw_ref[...] = neg_decay_log
    k_ref[...] = (k * (1.0 + (a_sig - 1.0) * ka_ref[...])).astype(k_ref.dtype)
    v_ref[...] = v.astype(v_ref.dtype)
    an_ref[...] = (-kk).astype(an_ref.dtype)
    b_ref[...] = (kk * a_sig).astype(b_ref.dtype)
    gate_ref[...] = gate.astype(gate_ref.dtype)


def _rwkv_prep(x, first, gain, mu, w0, a0, kkc, ka, wr, wk, wv, w1, w2, a1, a2, g1, g2, ones_blk, *, seq_len):
    t = x.shape[0]
    tm = min(TOKEN_TILE, t)
    kern = functools.partial(_rwkv_prep_kernel, tm=tm, seq_len=seq_len)
    row = lambda n: pl.BlockSpec((tm, n), lambda i: (i, 0))
    if seq_len >= tm:
        tiles_per_seq = seq_len // tm
        first_spec = pl.BlockSpec((1, 1, D_MODEL), lambda i: (i // tiles_per_seq, 0, 0))
        last_spec = first_spec
        last_shape = jax.ShapeDtypeStruct((t // seq_len, 1, D_MODEL), F32)
    else:
        first_spec = row(D_MODEL)
        last_spec = row(D_MODEL)
        last_shape = jax.ShapeDtypeStruct((t, D_MODEL), F32)
    vec = _const_spec((1, D_MODEL))
    wide = lambda dt: jax.ShapeDtypeStruct((t, D_MODEL), dt)
    outs = [wide(BF16), wide(F32), wide(BF16), wide(BF16), wide(BF16), wide(BF16), wide(BF16), last_shape]
    return pl.pallas_call(
        kern,
        grid=(t // tm,),
        in_specs=[row(D_MODEL), first_spec, vec, _const_spec((6, D_MODEL)), vec, vec, vec, vec,
                  _const_spec(wr.shape), _const_spec(wk.shape), _const_spec(wv.shape),
                  _const_spec(w1.shape), _const_spec(w2.shape), _const_spec(a1.shape), _const_spec(a2.shape),
                  _const_spec(g1.shape), _const_spec(g2.shape), _const_spec((MXU_DIM, MXU_DIM))],
        out_specs=[row(D_MODEL)] * 7 + [last_spec],
        out_shape=outs,
        scratch_shapes=[pltpu.VMEM((tm + 8, D_MODEL), F32)],
        compiler_params=_params("arbitrary"),
        name="rwkv_prep",
    )(x, first, gain, mu, w0, a0, kkc, ka, wr, wk, wv, w1, w2, a1, a2, g1, g2, ones_blk)


def _rwkv_scan_kernel(r_ref, lw_ref, k_ref, v_ref, a_ref, b_ref, s0_ref, y_ref, s1_ref, s_ref, *, c, nsub, nsteps):
    step = pl.program_id(1)
    n_pairs = RWKV_HEADS // 2
    two_c = 2 * c
    levels = c.bit_length() - 1

    @pl.when(step == 0)
    def _():
        s_ref[...] = s0_ref[0]

    lane = lax.broadcasted_iota(jnp.int32, (c, LANES), 1)
    m_a = lane < RWKV_HEAD
    r_i = lax.broadcasted_iota(jnp.int32, (two_c, two_c), 0)
    c_i = lax.broadcasted_iota(jnp.int32, (two_c, two_c), 1)
    strict = r_i > c_i
    s_r = lax.broadcasted_iota(jnp.int32, (LANES, LANES), 0)
    s_c = lax.broadcasted_iota(jnp.int32, (LANES, LANES), 1)
    blk = (s_r < RWKV_HEAD) == (s_c < RWKV_HEAD)
    r_i2 = lax.broadcasted_iota(jnp.int32, (two_c, 2 * two_c), 0)
    c_i2 = lax.broadcasted_iota(jnp.int32, (two_c, 2 * two_c), 1) % two_c
    incl2 = r_i2 >= c_i2
    t_r = lax.broadcasted_iota(jnp.int32, (c, c), 0)
    t_c = lax.broadcasted_iota(jnp.int32, (c, c), 1)
    tri = (t_r >= t_c).astype(F32)

    def stack(x):
        return jnp.concatenate([jnp.where(m_a, x, 0.0), jnp.where(m_a, 0.0, x)], axis=0)

    def sub_chunk(ci, carry):
        rows = pl.ds(pl.multiple_of(ci * c, c), c)
        lw = lw_ref[0, rows, :]
        cl = _dot(tri, lw, precision=HIGHEST)
        cl_end = cl[c - 1:c, :]
        e_neg = jnp.exp(-cl)
        e_end = jnp.exp(cl_end - cl)
        w_end = jnp.exp(cl_end)
        aa = a_ref[0, rows, :].astype(F32) * jnp.exp(cl - lw)
        rr = r_ref[0, rows, :].astype(F32) * jnp.exp(cl)
        bb = b_ref[0, rows, :].astype(F32)
        kk = k_ref[0, rows, :].astype(F32)
        vv = v_ref[0, rows, :].astype(F32)
        pairs = range(n_pairs)
        sl = [slice(p * LANES, (p + 1) * LANES) for p in pairs]
        lhs = [jnp.concatenate([stack(aa[:, sl[p]]), stack(rr[:, sl[p]])], axis=0).astype(BF16) for p in pairs]
        rhs = [jnp.concatenate([stack((bb * e_neg)[:, sl[p]]), stack((kk * e_neg)[:, sl[p]])], axis=0).astype(BF16)
               for p in pairs]
        bk = [jnp.concatenate([stack((bb * e_end)[:, sl[p]]), stack((kk * e_end)[:, sl[p]])], axis=0).astype(BF16)
              for p in pairs]
        v_s = [stack(vv[:, sl[p]]) for p in pairs]
        s_old = [s_ref[p] for p in pairs]
        gmat = [_dot_nt(lhs[p], rhs[p]) for p in pairs]
        ph = [_dot_nt(lhs[p], s_old[p].astype(BF16)) for p in pairs]
        npow = [jnp.where(strict, gmat[p][:two_c, :two_c], 0.0).astype(BF16) for p in pairs]
        a_ak = [jnp.where(strict, gmat[p][:two_c, two_c:], 0.0).astype(BF16) for p in pairs]
        a_r = [jnp.where(incl2, gmat[p][two_c:, :], 0.0).astype(BF16) for p in pairs]
        u = [ph[p][:two_c] + _dot(a_ak[p], v_s[p].astype(BF16)) for p in pairs]
        for lvl in range(levels - 1):
            prod = [_dot(npow[p], jnp.concatenate([u[p].astype(BF16), npow[p]], axis=1)) for p in pairs]
            u = [u[p] + prod[p][:, :LANES] for p in pairs]
            npow = [prod[p][:, LANES:].astype(BF16) for p in pairs]
        u = [u[p] + _dot(npow[p], u[p].astype(BF16)) for p in pairs]
        uv = [jnp.concatenate([u[p], v_s[p]], axis=0).astype(BF16) for p in pairs]
        y_s = [ph[p][two_c:] + _dot(a_r[p], uv[p]) for p in pairs]
        y_ref[0, rows, :] = jnp.concatenate([y_s[p][:c] + y_s[p][c:] for p in pairs], axis=1).astype(y_ref.dtype)
        for p in pairs:
            s_new = s_old[p] * w_end[:, sl[p]] + _dot_tn(uv[p], bk[p])
            s_ref[p] = jnp.where(blk, s_new, 0.0)
        return carry

    lax.fori_loop(0, nsub, sub_chunk, 0)

    @pl.when(step == nsteps - 1)
    def _():
        s1_ref[0] = s_ref[...]


def _rwkv_scan(r, lw, k, v, an, b, s0):
    bsz, lp, _ = r.shape
    c = min(RWKV_CHUNK, lp)
    rows = min(lp, 512)
    nsteps = lp // rows
    nsub = rows // c
    kern = functools.partial(_rwkv_scan_kernel, c=c, nsub=nsub, nsteps=nsteps)
    seq = pl.BlockSpec((1, rows, D_MODEL), lambda i, s: (i, s, 0))
    st = pl.BlockSpec((1, RWKV_HEADS // 2, LANES, LANES), lambda i, s: (i, 0, 0, 0))
    return pl.pallas_call(
        kern,
        grid=(bsz, nsteps),
        in_specs=[seq] * 6 + [st],
        out_specs=[seq, st],
        out_shape=[jax.ShapeDtypeStruct((bsz, lp, D_MODEL), BF16),
                   jax.ShapeDtypeStruct((bsz, RWKV_HEADS // 2, LANES, LANES), F32)],
        scratch_shapes=[pltpu.VMEM((RWKV_HEADS // 2, LANES, LANES), F32)],
        compiler_params=_params("arbitrary", "arbitrary"),
        name="rwkv_scan",
    )(r, lw, k, v, an, b, s0)


def _route_top2(h, router_hi, router_lo):
    h_hi = h.astype(BF16)
    h_lo = (h - h_hi.astype(F32)).astype(BF16)
    logits = _dot(h_hi, router_hi) + (_dot(h_hi, router_lo) + _dot(h_lo, router_hi))
    lane = lax.broadcasted_iota(jnp.int32, logits.shape, 1)
    logits = jnp.where(lane < N_EXPERTS, logits, -jnp.inf)
    m1 = jnp.max(logits, axis=-1, keepdims=True)
    i1 = jnp.min(jnp.where(logits == m1, lane, LANES), axis=-1, keepdims=True)
    rest = jnp.where(lane == i1, -jnp.inf, logits)
    m2 = jnp.max(rest, axis=-1, keepdims=True)
    i2 = jnp.min(jnp.where(rest == m2, lane, LANES), axis=-1, keepdims=True)
    e2 = jnp.exp(m2 - m1)
    g1 = 1.0 / (1.0 + e2)
    g2 = e2 / (1.0 + e2)
    comb = jnp.where(lane == i1, g1, 0.0) + jnp.where(lane == i2, g2, 0.0)
    mask = jnp.where((lane == i1) | (lane == i2), 1.0, 0.0)
    return comb, mask


def _rwkv_out_kernel(x_ref, y_ref, r_ref, k_ref, v_ref, gate_ref, lnw_ref, lnb_ref, rk_ref, wo_ref, ones_ref, o_ref):
    ones_blk = ones_ref[...]
    inv = 1.0 / RWKV_HEAD
    parts = []
    for c in range(D_MODEL // MXU_DIM):
        sl = slice(c * MXU_DIM, (c + 1) * MXU_DIM)
        yh = y_ref[:, sl]
        y = yh.astype(F32)
        yc = y - _dot(yh, ones_blk) * inv
        var = _head_sums(yc * yc, ones_blk) * inv
        yn = yc * lax.rsqrt(var + RWKV_GN_EPS) * lnw_ref[:, sl] + lnb_ref[:, sl]
        rk = r_ref[:, sl].astype(F32) * k_ref[:, sl].astype(F32) * rk_ref[:, sl]
        rkh = rk.astype(BF16)
        bonus = _dot(rkh, ones_blk) + _head_sums(rk - rkh.astype(F32), ones_blk)
        parts.append(yn + bonus * v_ref[:, sl].astype(F32))
    out = jnp.concatenate(parts, axis=1) * gate_ref[...].astype(F32)
    o_ref[...] = x_ref[...] + _dot(out.astype(BF16), wo_ref[...])


def _rwkv_out(x, y, r, k, v, gate, lnw, lnb, rk, wo, ones_blk):
    t = x.shape[0]
    tm = min(TOKEN_TILE, t)
    row = pl.BlockSpec((tm, D_MODEL), lambda i: (i, 0))
    vec = _const_spec((1, D_MODEL))
    return pl.pallas_call(
        _rwkv_out_kernel,
        grid=(t // tm,),
        in_specs=[row] * 6 + [vec, vec, vec, _const_spec(wo.shape), _const_spec((MXU_DIM, MXU_DIM))],
        out_specs=row,
        out_shape=jax.ShapeDtypeStruct((t, D_MODEL), F32),
        compiler_params=_params("arbitrary"),
        name="rwkv_out",
    )(x, y, r, k, v, gate, lnw, lnb, rk, wo, ones_blk)


def _moe_route_kernel(x_ref, g_ref, rhi_ref, rlo_ref, comb_ref, mask_ref):
    comb_ref[...], mask_ref[...] = _route_top2(_rms_rows(x_ref[...], g_ref[...]), rhi_ref[...], rlo_ref[...])


def _moe_route(x, gain, router):
    t = x.shape[0]
    tm = min(TOKEN_TILE, t)
    row = lambda n: pl.BlockSpec((tm, n), lambda i: (i, 0))
    router_hi = router.astype(BF16)
    router_lo = (router - router_hi.astype(F32)).astype(BF16)
    return pl.pallas_call(
        _moe_route_kernel,
        grid=(t // tm,),
        in_specs=[row(D_MODEL), _const_spec((1, D_MODEL)), _const_spec((D_MODEL, LANES)),
                  _const_spec((D_MODEL, LANES))],
        out_specs=[row(LANES), row(LANES)],
        out_shape=[jax.ShapeDtypeStruct((t, LANES), F32), jax.ShapeDtypeStruct((t, LANES), F32)],
        compiler_params=_params("arbitrary"),
        name="moe_route",
    )(x, gain, router_hi, router_lo)


def _moe_plan(mask, comb, tm):
    m = mask[:, :N_EXPERTS].astype(jnp.int32)
    t = m.shape[0]
    tt = min(TOKEN_TILE, t)
    n_tiles = (2 * t) // tm + N_EXPERTS
    rank = jnp.cumsum(m, axis=0)
    counts = rank[-1]
    tiles_e = (counts + tm - 1) // tm
    tile_end = jnp.cumsum(tiles_e)
    tile_start = tile_end - tiles_e
    pos = (tile_start * tm)[None, :] + rank - 1
    slot = jnp.cumsum(m, axis=1) - 1
    c8 = comb[:, :N_EXPERTS]
    pick = [(slot == k) & (m > 0) for k in range(2)]
    pos_k = jnp.stack([jnp.sum(jnp.where(pk, pos, 0), axis=1) for pk in pick], axis=0)
    gates = jnp.stack([jnp.sum(jnp.where(pk, c8, 0.0), axis=1) for pk in pick], axis=1)
    pos_tab = pos_k.reshape(2, t // tt, tt).transpose(1, 0, 2).reshape(t // tt, 1, 2 * tt)
    tile_ids = jnp.arange(n_tiles, dtype=jnp.int32)
    texp = jnp.sum((tile_ids[:, None] >= tile_end[None, :]).astype(jnp.int32), axis=1)
    texp_c = jnp.minimum(texp, N_EXPERTS - 1)
    nvalid = jnp.clip(counts[texp_c] - (tile_ids - tile_start[texp_c]) * tm, 0, tm)
    nvalid = jnp.where(texp < N_EXPERTS, nvalid, 0).astype(jnp.int32)
    last_tile = jnp.concatenate([jnp.where(tiles_e > 0, tile_end - 1, -1), tile_end[-1:]]).astype(jnp.int32)
    return texp_c.astype(jnp.int32), nvalid, last_tile, pos_tab.astype(jnp.int32), gates, n_tiles


def _start_row_copies(pos_ref, tt, make):
    def body(j, c):
        for k in range(2):
            make(k, j, pos_ref[0, 0, k * tt + j]).start(priority=k)
        return c
    lax.fori_loop(0, tt, body, 0, unroll=8)


def _moe_dispatch_kernel(last_ref, pos_ref, x_ref, g_ref, xs_hbm, hbuf, zbuf, sem, zsem, *, tt, tm, n_tiles, nsteps):
    i = pl.program_id(0)

    def zero_tile(tile):
        return pltpu.make_async_copy(zbuf, xs_hbm.at[pl.ds(pl.multiple_of(tile * tm, tm), tm), :], zsem)

    @pl.when(i == 0)
    def _():
        zbuf[...] = jnp.zeros(zbuf.shape, F32)
        for e in range(N_EXPERTS):
            @pl.when(last_ref[e] >= 0)
            def _():
                zero_tile(last_ref[e]).start()
        for e in range(N_EXPERTS):
            @pl.when(last_ref[e] >= 0)
            def _():
                zero_tile(last_ref[e]).wait()

        def clear_unused(tile, c):
            zero_tile(tile).start()
            zero_tile(tile).wait()
            return c
        lax.fori_loop(last_ref[N_EXPERTS], n_tiles, clear_unused, 0)

    def drain(s):
        for _ in range(2):
            pltpu.make_async_copy(hbuf.at[s], xs_hbm.at[pl.ds(0, tt), :], sem.at[s]).wait()

    for s in range(2):
        @pl.when((i % 2 == s) & (i >= 2))
        def _():
            drain(s)

        @pl.when(i % 2 == s)
        def _():
            hbuf[s] = _rms_rows(x_ref[...], g_ref[...])
            _start_row_copies(pos_ref, tt, lambda k, j, dst: pltpu.make_async_copy(
                hbuf.at[s, pl.ds(j, 1), :], xs_hbm.at[pl.ds(dst, 1), :], sem.at[s]))

    @pl.when(i == nsteps - 1)
    def _():
        for s in range(min(2, nsteps)):
            drain(s)


def _moe_dispatch(last_tile, pos_tab, x, gain, n_rows, tm):
    nsteps, _, tt2 = pos_tab.shape
    tt = tt2 // 2
    kern = functools.partial(_moe_dispatch_kernel, tt=tt, tm=tm, n_tiles=n_rows // tm, nsteps=nsteps)
    return pl.pallas_call(
        kern,
        grid_spec=pltpu.PrefetchScalarGridSpec(
            num_scalar_prefetch=1,
            grid=(nsteps,),
            in_specs=[pl.BlockSpec((1, 1, tt2), lambda i, lt: (i, 0, 0), memory_space=pltpu.SMEM),
                      pl.BlockSpec((tt, D_MODEL), lambda i, lt: (i, 0)),
                      pl.BlockSpec((1, D_MODEL), lambda i, lt: (0, 0))],
            out_specs=pl.BlockSpec(memory_space=pl.ANY),
            scratch_shapes=[pltpu.VMEM((2, tt, D_MODEL), F32), pltpu.VMEM((tm, D_MODEL), F32),
                            pltpu.SemaphoreType.DMA((2,)), pltpu.SemaphoreType.DMA(())],
        ),
        out_shape=jax.ShapeDtypeStruct((n_rows, D_MODEL), F32),
        compiler_params=_params("arbitrary"),
        name="moe_dispatch",
    )(last_tile, pos_tab, x, gain)


def _moe_ffn_kernel(texp_ref, nvalid_ref, xs_ref, wg_ref, wu_ref, wd_ref, ys_ref, act_ref):
    n = nvalid_ref[pl.program_id(0)]

    @pl.when(n > 0)
    def _():
        _swiglu_act(xs_ref[...].astype(BF16), wg_ref.at[0], wu_ref.at[0], act_ref)
        ys_ref[...] = _dot(act_ref[...], wd_ref[0])

    @pl.when(n == 0)
    def _():
        ys_ref[...] = jnp.zeros(ys_ref.shape, F32)


def _moe_ffn(texp, nvalid, xs, wg, wu, wd, tm):
    n_tiles = xs.shape[0] // tm
    wspec = lambda s: pl.BlockSpec((1,) + s, lambda i, te, nv: (te[i], 0, 0))
    row = pl.BlockSpec((tm, D_MODEL), lambda i, te, nv: (i, 0))
    return pl.pallas_call(
        _moe_ffn_kernel,
        grid_spec=pltpu.PrefetchScalarGridSpec(
            num_scalar_prefetch=2,
            grid=(n_tiles,),
            in_specs=[row, wspec((D_MODEL, D_FF)), wspec((D_MODEL, D_FF)), wspec((D_FF, D_MODEL))],
            out_specs=row,
            scratch_shapes=[pltpu.VMEM((tm, D_FF), BF16)],
        ),
        out_shape=jax.ShapeDtypeStruct(xs.shape, F32),
        compiler_params=_params("arbitrary"),
        name="moe_ffn",
    )(texp, nvalid, xs, wg, wu, wd)


def _moe_combine_kernel(pos_ref, next_ref, x_ref, g_ref, ys_hbm, o_ref, ybuf, sem, *, tt, nsteps):
    i = pl.program_id(0)

    def fetch(table_ref, s):
        _start_row_copies(table_ref, tt, lambda k, j, src: pltpu.make_async_copy(
            ys_hbm.at[pl.ds(src, 1), :], ybuf.at[s, k, pl.ds(j, 1), :], sem.at[s]))

    @pl.when(i == 0)
    def _():
        fetch(pos_ref, 0)

    for s in range(2):
        @pl.when((i % 2 == s) & (i + 1 < nsteps))
        def _():
            fetch(next_ref, 1 - s)

        @pl.when(i % 2 == s)
        def _():
            for k in range(2):
                pltpu.make_async_copy(ys_hbm.at[pl.ds(0, tt), :], ybuf.at[s, k], sem.at[s]).wait()
            g = g_ref[...]
            o_ref[...] = x_ref[...] + (g[:, 0:1] * ybuf[s, 0] + g[:, 1:2] * ybuf[s, 1])


def _moe_combine(pos_tab, x, gates, ys):
    t = x.shape[0]
    nsteps, _, tt2 = pos_tab.shape
    tt = tt2 // 2
    kern = functools.partial(_moe_combine_kernel, tt=tt, nsteps=nsteps)
    row = lambda n: pl.BlockSpec((tt, n), lambda i: (i, 0))
    table = lambda off: pl.BlockSpec((1, 1, tt2), lambda i: (jnp.minimum(i + off, nsteps - 1), 0, 0),
                                     memory_space=pltpu.SMEM)
    return pl.pallas_call(
        kern,
        grid=(nsteps,),
        in_specs=[table(0), table(1), row(D_MODEL), row(2), pl.BlockSpec(memory_space=pl.ANY)],
        out_specs=row(D_MODEL),
        out_shape=jax.ShapeDtypeStruct((t, D_MODEL), F32),
        scratch_shapes=[pltpu.VMEM((2, 2, tt, D_MODEL), F32), pltpu.SemaphoreType.DMA((2,))],
        compiler_params=_params("arbitrary"),
        name="moe_combine",
    )(pos_tab, pos_tab, x, gates, ys)


def _moe(x, gain, router, wg, wu, wd):
    t = x.shape[0]
    tm = TOKEN_TILE if 2 * t >= N_EXPERTS * 4 * TOKEN_TILE else TOKEN_TILE // 2
    comb, mask = _moe_route(x, gain, router)
    texp, nvalid, last_tile, pos_tab, gates, n_tiles = _moe_plan(mask, comb, tm)
    xs = _moe_dispatch(last_tile, pos_tab, x, gain, n_tiles * tm, tm)
    ys = _moe_ffn(texp, nvalid, xs, wg, wu, wd, tm)
    return _moe_combine(pos_tab, x, gates, ys)


def _pair_states(s):
    b = s.shape[0]
    s = s.reshape(b, RWKV_HEADS // 2, 2, RWKV_HEAD, RWKV_HEAD)
    z = jnp.zeros_like(s[:, :, 0])
    top = jnp.concatenate([s[:, :, 0], z], axis=-1)
    bot = jnp.concatenate([z, s[:, :, 1]], axis=-1)
    return jnp.concatenate([top, bot], axis=-2)


def _unpair_states(s):
    b = s.shape[0]
    s0 = s[:, :, :RWKV_HEAD, :RWKV_HEAD]
    s1 = s[:, :, RWKV_HEAD:, RWKV_HEAD:]
    return jnp.stack([s0, s1], axis=2).reshape(b, RWKV_HEADS, RWKV_HEAD, RWKV_HEAD)


def _trunk(x, ssm, conv, ck, cv, wkv, shift, w):
    b, l, _ = x.shape
    t = b * l
    ones_blk = w["ones_blk"]
    xf = x.reshape(t, D_MODEL)

    z, xbc, dt, q, k, v = _inproj(xf, w["norm_mix0"], w["wz"], w["wxbc"], w["wdt"], w["wq"], w["wk"], w["wv"],
                                  w["q_norm"], w["k_norm"], ones_blk)
    lp = l if l % SSD_CHUNK == 0 else 8
    seq = lambda a: a.reshape(b, l, a.shape[-1])
    padl = lambda a: a if lp == l else jnp.pad(a, ((0, 0), (0, lp - l), (0, 0)))
    y_ssd, ssm1, conv1 = _ssd(padl(seq(xbc)), padl(seq(dt)), padl(seq(z)), conv, ssm.reshape(b, SSD_WIDTH, SSD_STATE),
                              w["conv_w"], w["conv_b"], w["dt_bias"], w["a_log"], w["d_skip"], w["ssd_norm"], lv=l)
    y_ssd = y_ssd[:, :l].reshape(t, SSD_WIDTH)
    ssm1 = ssm1.reshape(b, SSD_HEADS, SSD_HEAD_DIM, SSD_STATE)
    if ck is None:
        o = _swa_prompt(seq(q), seq(k), seq(v), w["sinks"]).reshape(t, Q_DIM)
        k1 = seq(k)[:, -WINDOW:].reshape(b, WINDOW, ATT_KV_HEADS, ATT_HEAD_DIM)
        v1 = seq(v)[:, -WINDOW:].reshape(b, WINDOW, ATT_KV_HEADS, ATT_HEAD_DIM)
    else:
        wc = ck.shape[1]
        o, k1, v1 = _swa_cached(q, k, v, ck.reshape(b, wc, KV_DIM), cv.reshape(b, wc, KV_DIM), w["sinks"], l=l)
        k1 = k1.reshape(b, wc, ATT_KV_HEADS, ATT_HEAD_DIM)
        v1 = v1.reshape(b, wc, ATT_KV_HEADS, ATT_HEAD_DIM)
    x2 = _outproj_ffn(xf, y_ssd, o, w["wo_a"], w["wo_b"], w["norm_ffn0"], w["ffn_gate"], w["ffn_up"], w["ffn_down"])

    tm = min(TOKEN_TILE, t)
    if l >= tm:
        first = shift.reshape(b, 1, D_MODEL)
    else:
        first = jnp.repeat(shift, l, axis=0)
    r, lw, kx, vx, an, bb, gate, h1 = _rwkv_prep(
        x2, first, w["norm_mix1"], w["mu"], w["w0"], w["a0"], w["kk"], w["ka"], w["wr"], w["wkk"], w["wvv"],
        w["w1"], w["w2"], w["a1"], w["a2"], w["g1"], w["g2"], ones_blk, seq_len=l)
    shift1 = h1.reshape(b, D_MODEL) if l >= tm else h1.reshape(b, l, D_MODEL)[:, -1]
    if l >= RWKV_CHUNK:
        lpr = -(-l // RWKV_CHUNK) * RWKV_CHUNK
    else:
        lpr = max(RWKV_SHORT_CHUNK, pl.next_power_of_2(l))
    padr = lambda a: seq(a) if lpr == l else jnp.pad(seq(a), ((0, 0), (0, lpr - l), (0, 0)))
    y, s1 = _rwkv_scan(padr(r), padr(lw), padr(kx), padr(vx), padr(an), padr(bb), _pair_states(wkv))
    y = y[:, :l].reshape(t, D_MODEL)
    x3 = _rwkv_out(x2, y, r, kx, vx, gate, w["ln_w"], w["ln_b"], w["rk"], w["wo"], ones_blk)
    x4 = _moe(x3, w["norm_ffn1"], w["router"], w["moe_gate"], w["moe_up"], w["moe_down"])
    return (x4.reshape(b, l, D_MODEL), ssm1[None], conv1[None], k1[None], v1[None],
            _unpair_states(s1)[None], shift1[None])


def kernel(x_prompt, x_sample, state_ssm, state_conv, cache_swa_k, cache_swa_v, state_wkv, state_shift, norm_mix, norm_ffn, w_in, conv_w, conv_b, dt_bias, a_log, d_skip, ssd_norm, q_norm, k_norm, attn_sinks, w_out, ffn_gate, ffn_up, ffn_down, rwkv_mu, rwkv_w0, rwkv_w1, rwkv_w2, rwkv_a0, rwkv_a1, rwkv_a2, rwkv_g1, rwkv_g2, rwkv_kk, rwkv_ka, rwkv_rk, rwkv_wr, rwkv_wk, rwkv_wv, rwkv_wo, rwkv_ln_w, rwkv_ln_b, moe_router, moe_gate, moe_up, moe_down):
    bf = lambda a: a.astype(BF16)
    row = lambda a: a.reshape(1, -1).astype(F32)
    padlane = lambda a: jnp.pad(a, ((0, 0), (0, LANES - a.shape[1])))
    wi = w_in[0]
    c0 = SSD_WIDTH
    c1 = c0 + CONV_DIM
    c2 = c1 + SSD_HEADS
    c3 = c2 + Q_DIM
    c4 = c3 + KV_DIM
    w = dict(
        ones_blk=_block_ones(),
        norm_mix0=row(norm_mix[0]), norm_mix1=row(norm_mix[1]), norm_ffn0=row(norm_ffn[0]), norm_ffn1=row(norm_ffn[1]),
        wz=bf(wi[:, :c0]), wxbc=bf(wi[:, c0:c1]), wdt=bf(padlane(wi[:, c1:c2])), wq=bf(wi[:, c2:c3]),
        wk=bf(wi[:, c3:c4]), wv=bf(wi[:, c4:]),
        q_norm=row(jnp.tile(q_norm[0], ATT_HEADS)), k_norm=row(jnp.tile(k_norm[0], ATT_KV_HEADS)),
        conv_w=conv_w[0], conv_b=row(conv_b[0]), dt_bias=padlane(row(dt_bias[0])), a_log=padlane(row(a_log[0])),
        d_skip=row(jnp.repeat(d_skip[0], SSD_HEAD_DIM)), ssd_norm=row(ssd_norm[0]), sinks=attn_sinks[0].astype(F32),
        wo_a=bf(w_out[0, :SSD_WIDTH]), wo_b=bf(w_out[0, SSD_WIDTH:]),
        ffn_gate=bf(ffn_gate[0]), ffn_up=bf(ffn_up[0]), ffn_down=bf(ffn_down[0]),
        mu=rwkv_mu[0], w0=row(rwkv_w0[0]), a0=row(rwkv_a0[0]), kk=row(rwkv_kk[0]), ka=row(rwkv_ka[0]),
        wr=bf(rwkv_wr[0]), wkk=bf(rwkv_wk[0]), wvv=bf(rwkv_wv[0]), wo=bf(rwkv_wo[0]),
        w1=bf(rwkv_w1[0]), w2=bf(rwkv_w2[0]), a1=bf(rwkv_a1[0]), a2=bf(rwkv_a2[0]),
        g1=bf(rwkv_g1[0]), g2=bf(rwkv_g2[0]),
        ln_w=row(rwkv_ln_w[0]), ln_b=row(rwkv_ln_b[0]), rk=row(rwkv_rk[0]),
        router=padlane(moe_router[0]), moe_gate=bf(moe_gate[0]), moe_up=bf(moe_up[0]), moe_down=bf(moe_down[0]),
    )
    bp = x_prompt.shape[0]
    z_ssm = jnp.zeros((bp,) + state_ssm.shape[2:], F32)
    z_conv = jnp.zeros((bp,) + state_conv.shape[2:], F32)
    z_wkv = jnp.zeros((bp,) + state_wkv.shape[2:], F32)
    z_shift = jnp.zeros((bp,) + state_shift.shape[2:], F32)
    outs_p = _trunk(x_prompt, z_ssm, z_conv, None, None, z_wkv, z_shift, w)
    outs_s = _trunk(x_sample, state_ssm[0], state_conv[0], cache_swa_k[0], cache_swa_v[0], state_wkv[0],
                    state_shift[0], w)
    return (outs_p[0], outs_s[0]) + tuple(outs_p[1:]) + tuple(outs_s[1:])
```
